```python
import functools
import jax
import jax.numpy as jnp
from jax import lax
import numpy as np

D_MODEL = 1024
BATCH = 4
SEQ = 4096
DEPTH = 2
DEC_BATCH = 32
DEC_SEQ = 1
PAST_LEN = 8192
PAGE_SIZE = 128

N_PAGES = PAST_LEN // PAGE_SIZE

RG_WIDTH = D_MODEL
RG_BLOCKS = 8
RG_BLOCK_W = RG_WIDTH // RG_BLOCKS
RG_CONV_W = 4
RG_C = 8.0
GLA_HEADS = 4
GLA_DK = D_MODEL // (2 * GLA_HEADS)
GLA_DV = D_MODEL // GLA_HEADS
GLA_RANK = 16
GLA_TAU = 16.0
GLA_CHUNK = 64
NSA_HEADS = 16
NSA_KV_HEADS = 4
NSA_HPG = NSA_HEADS // NSA_KV_HEADS
NSA_DH = D_MODEL // NSA_HEADS
CMP_BLOCK = 32
CMP_STRIDE = 16
CMP_RATIO = CMP_BLOCK // CMP_STRIDE
SEL_BLOCK = 64
TOP_N = 16
WINDOW = 512
NSA_Q_BLOCK = 64
D_FF = ((8 * D_MODEL + 3 * 256 - 1) // (3 * 256)) * 256

KV_W = 2 * NSA_KV_HEADS * NSA_DH
IN_SIZES = (RG_WIDTH, RG_WIDTH,
            GLA_HEADS * GLA_DK, GLA_HEADS * GLA_DK, GLA_HEADS * GLA_DV, GLA_HEADS * GLA_DV, GLA_RANK,
            NSA_HEADS * NSA_DH, KV_W, KV_W, KV_W, 3 * NSA_HEADS,
            3 * D_MODEL)
D_IN = sum(IN_SIZES)

EPS = 1e-6
NEG = -1e30
TINY = 1e-30
FORCE_SCORE = 1e4

kernel_name = 'hybrid_rglru_gla_nsa_decoder_step'


def split_cols(z):
    outs, start = [], 0
    for n in IN_SIZES:
        outs.append(z[..., start:start + n])
        start += n
    return outs


def rms_norm(x, g):
    xf = x.astype(jnp.float32)
    y = xf * lax.rsqrt(jnp.mean(xf * xf, axis=-1, keepdims=True) + EPS)
    return (y * g.astype(jnp.float32)).astype(x.dtype)


def norm_keys(kv, g):
    return jnp.stack([rms_norm(kv[:, :, 0], g), kv[:, :, 1]], axis=2)


def masked_softmax(s, mask):
    s = jnp.where(mask, s.astype(jnp.float32), NEG)
    m = jnp.max(s, axis=-1, keepdims=True)
    e = jnp.where(mask, jnp.exp(s - m), 0.0)
    return e / jnp.maximum(jnp.sum(e, axis=-1, keepdims=True), TINY)


def alibi_slopes():
    h = jnp.arange(1, NSA_HEADS + 1, dtype=jnp.float32)
    return (2.0 ** (-8.0 * h / NSA_HEADS)).reshape(NSA_KV_HEADS, NSA_HPG)


def causal_conv(x, buf, w, b):
    T = x.shape[1]
    xp = jnp.concatenate([buf.astype(x.dtype), x], axis=1)
    y = b
    for j in range(RG_CONV_W):
        y = y + xp[:, j:j + T] * w[j]
    return y, xp[:, -(RG_CONV_W - 1):]


def rglru(x, h0, w_a, b_a, w_x, b_x, lam):
    B, T, C = x.shape
    f32 = jnp.float32
    xf = x.astype(f32)
    xb = xf.reshape(B, T, RG_BLOCKS, RG_BLOCK_W)
    rec = jax.nn.sigmoid(jnp.einsum('btnc,ncd->btnd', xb, w_a.astype(f32)).reshape(B, T, C) + b_a.astype(f32))
    inp = jax.nn.sigmoid(jnp.einsum('btnc,ncd->btnd', xb, w_x.astype(f32)).reshape(B, T, C) + b_x.astype(f32))
    log_a = -RG_C * rec * jax.nn.softplus(-lam.astype(f32))
    a = jnp.exp(log_a)
    u = jnp.sqrt(-jnp.expm1(2.0 * log_a)) * (inp * xf)
    u = u.at[:, 0].add(a[:, 0] * h0.astype(f32))

    def combine(left, right):
        a1, b1 = left
        a2, b2 = right
        return a1 * a2, a2 * b1 + b2

    _, h = lax.associative_scan(combine, (a, u), axis=1)
    return h.astype(x.dtype), h[:, -1]


def gla_recurrent(q, k, v, log_g, s0):
    B, T, H, DK = q.shape
    f32 = jnp.float32
    C = min(GLA_CHUNK, T)
    n_ch = -(-T // C)
    pad = n_ch * C - T

    def prep(a):
        a = jnp.pad(a.astype(f32), ((0, 0), (0, pad), (0, 0), (0, 0)))
        return a.reshape(B, n_ch, C, H, a.shape[-1]).transpose(1, 0, 3, 2, 4)

    causal = jnp.tril(jnp.ones((C, C), dtype=bool))

    def step(S, inp):
        qc, kc, vc, gc = inp
        b = jnp.cumsum(gc, axis=2)
        diff = b[:, :, :, None, :] - b[:, :, None, :, :]
        decay = jnp.exp(jnp.where(causal[:, :, None], diff, -jnp.inf))
        A = jnp.einsum('bhtd,bhsd,bhtsd->bhts', qc, kc, decay)
        o = jnp.einsum('bhts,bhsv->bhtv', A, vc) + jnp.einsum('bhtd,bhdv->bhtv', qc * jnp.exp(b), S)
        b_last = b[:, :, -1]
        S = jnp.exp(b_last)[..., None] * S + jnp.einsum(
            'bhsd,bhsv->bhdv', kc * jnp.exp(b_last[:, :, None] - b), vc)
        return S, o

    S, o = lax.scan(step, s0.astype(f32), (prep(q), prep(k), prep(v), prep(log_g)))
    o = o.transpose(1, 0, 3, 2, 4).reshape(B, n_ch * C, H, -1)[:, :T]
    return o, S


def compress(rows, phi, pe):
    B, L = rows.shape[:2]
    n_chunk = L // CMP_STRIDE
    n_cmp = n_chunk - CMP_RATIO + 1
    ch = rows[:, :n_chunk * CMP_STRIDE].reshape(B, n_chunk, CMP_STRIDE, NSA_KV_HEADS, NSA_DH)
    out = 0.0
    for r in range(CMP_RATIO):
        sl = slice(r * CMP_STRIDE, (r + 1) * CMP_STRIDE)
        part = jnp.einsum('bnsgd,sde->bnge', ch + pe[sl][:, None, :], phi[sl])
        out = out + part[:, r:r + n_cmp]
    return out


def compressed_keys(cmp_kv, lw):
    kc = rms_norm(compress(cmp_kv[:, :, 0], lw['phi_k'], lw['cmp_pe'][0]), lw['k_norm'][0])
    vc = compress(cmp_kv[:, :, 1], lw['phi_v'], lw['cmp_pe'][1])
    c_end = jnp.arange(kc.shape[1]) * CMP_STRIDE + (CMP_BLOCK - 1)
    return kc, vc, c_end


def cmp_to_sel(n_cmp, n_sel):
    start = jnp.arange(n_cmp)[:, None] * CMP_STRIDE
    sel = jnp.arange(n_sel)[None, :] * SEL_BLOCK
    return ((start < sel + SEL_BLOCK) & (start + CMP_BLOCK > sel)).astype(jnp.float32)


def nsa_attend(q, q_pos, kc, vc, c_end, n_sel, fetch, k_win, v_win, w_pos, gates, slopes):
    f32 = jnp.float32
    sl = slopes[None, :, :, None, None]
    s = jnp.einsum('bqgpd,bngd->bgpqn', q, kc).astype(f32)
    dist = (q_pos[:, None] - c_end[None, :]).astype(f32)
    p_cmp = masked_softmax(s - sl * dist, c_end[None, :] <= q_pos[:, None])
    o_cmp = jnp.einsum('bgpqn,bngd->bqgpd', p_cmp.astype(vc.dtype), vc)
    imp = jnp.einsum('bgpqn,nj->bgqj', p_cmp, cmp_to_sel(kc.shape[1], n_sel))
    j = jnp.arange(n_sel)[None, :]
    cur = (q_pos // SEL_BLOCK)[:, None]
    valid = j <= cur
    forced = (j == 0) | (j == cur) | (j == cur - 1)
    score = jnp.where(valid, jnp.where(forced, FORCE_SCORE, imp), NEG)
    _, idx = lax.top_k(score, min(TOP_N, n_sel))
    Bq, G, Q, n = idx.shape
    kg, vg = fetch(idx)
    kg = kg.reshape(Bq, G, Q, n * SEL_BLOCK, NSA_DH)
    vg = vg.reshape(Bq, G, Q, n * SEL_BLOCK, NSA_DH)
    k_pos = (idx[..., None] * SEL_BLOCK + jnp.arange(SEL_BLOCK)).reshape(Bq, G, Q, n * SEL_BLOCK)
    s = jnp.einsum('bqgpd,bgqkd->bgpqk', q, kg).astype(f32)
    dist = (q_pos[None, None, :, None] - k_pos)[:, :, None]
    p = masked_softmax(s - sl * dist.astype(f32), dist >= 0)
    o_slc = jnp.einsum('bgpqk,bgqkd->bqgpd', p.astype(vg.dtype), vg)
    s = jnp.einsum('bqgpd,bwgd->bgpqw', q, k_win).astype(f32)
    dist = q_pos[:, None] - w_pos[None, :]
    mask = (dist >= 0) & (dist <= WINDOW) & (w_pos[None, :] >= 0)
    p = masked_softmax(s - sl * dist.astype(f32), mask)
    o_win = jnp.einsum('bgpqw,bwgd->bqgpd', p.astype(v_win.dtype), v_win)
    return gates[..., 0:1] * o_cmp + gates[..., 1:2] * o_slc + gates[..., 2:3] * o_win


def nsa_prompt(q, cmp_kv, slc_kv, win_kv, gates, lw, slopes):
    B, T = q.shape[:2]
    G, DH = NSA_KV_HEADS, NSA_DH
    kc, vc, c_end = compressed_keys(cmp_kv, lw)
    n_sel = T // SEL_BLOCK
    blocks = slc_kv.reshape(B, n_sel, SEL_BLOCK, 2, G, DH)
    b_idx = jnp.arange(B)[:, None, None, None]
    g_idx = jnp.arange(G)[None, :, None, None]

    def fetch(idx):
        kv = blocks[b_idx, idx, :, :, g_idx]
        return kv[..., 0, :], kv[..., 1, :]

    win_pad = jnp.pad(win_kv, ((0, 0), (WINDOW, 0), (0, 0), (0, 0), (0, 0)))
    QB = min(NSA_Q_BLOCK, T)
    nqb = T // QB

    def blockify(a):
        return a.reshape(B, nqb, QB, *a.shape[2:]).swapaxes(0, 1)

    def one_block(inp):
        i, qb, gb = inp
        start = i * QB
        q_pos = start + jnp.arange(QB)
        band = lax.dynamic_slice_in_dim(win_pad, start, WINDOW + QB, axis=1)
        w_pos = start - WINDOW + jnp.arange(WINDOW + QB)
        return nsa_attend(qb, q_pos, kc, vc, c_end, n_sel, fetch,
                          band[:, :, 0], band[:, :, 1], w_pos, gb, slopes)

    o = lax.map(one_block, (jnp.arange(nqb), blockify(q), blockify(gates)))
    o = o.swapaxes(0, 1).reshape(B, T, NSA_HEADS * DH)
    return o, win_kv[:, -min(WINDOW, T):]


def nsa_sample(q, cmp_kv, slc_kv, win_kv, gates, lw, slopes, cmp_pool, slc_pool, win_buf, page_table):
    B, S = q.shape[:2]
    G, DH = NSA_KV_HEADS, NSA_DH
    past_len = page_table.shape[1] * PAGE_SIZE
    past_cmp = cmp_pool[page_table].reshape(B, past_len, 2, G, DH).astype(cmp_kv.dtype)
    kc, vc, c_end = compressed_keys(jnp.concatenate([past_cmp, cmp_kv], axis=1), lw)
    n_past = past_len // SEL_BLOCK
    n_tail = -(-S // SEL_BLOCK)
    tail = jnp.pad(slc_kv, ((0, 0), (0, n_tail * SEL_BLOCK - S), (0, 0), (0, 0), (0, 0)))
    tail = tail.reshape(B, n_tail, SEL_BLOCK, 2, G, DH)
    bpp = PAGE_SIZE // SEL_BLOCK
    pool_blocks = slc_pool.reshape(-1, SEL_BLOCK, 2, G, DH)
    b_idx = jnp.arange(B)[:, None, None, None]
    g_idx = jnp.arange(G)[None, :, None, None]

    def fetch(idx):
        jp = jnp.minimum(idx, n_past - 1)
        phys = page_table[b_idx, jp // bpp] * bpp + jp % bpp
        past = pool_blocks[phys, :, :, g_idx].astype(slc_kv.dtype)
        fresh = tail[b_idx, jnp.clip(idx - n_past, 0, n_tail - 1), :, :, g_idx]
        kv = jnp.where((idx < n_past)[..., None, None, None], past, fresh)
        return kv[..., 0, :], kv[..., 1, :]

    win_all = jnp.concatenate([win_buf.astype(win_kv.dtype), win_kv], axis=1)
    w_len = win_buf.shape[1]
    w_pos = past_len - w_len + jnp.arange(w_len + S)
    q_pos = past_len + jnp.arange(S)
    o = nsa_attend(q, q_pos, kc, vc, c_end, n_past + n_tail, fetch,
                   win_all[:, :, 0], win_all[:, :, 1], w_pos, gates, slopes)
    return o.reshape(B, S, NSA_HEADS * DH), win_all[:, -w_len:]


def layer_forward(x, lw, nsa_fn, conv_buf, h0, s0):
    B, T, _ = x.shape
    G, P, DH = NSA_KV_HEADS, NSA_HPG, NSA_DH
    h = rms_norm(x, lw['norm_mix'])
    (rg_y, rg_x, g_q, g_k, g_v, g_r, g_a,
     n_q, n_cmp, n_slc, n_win, n_gate, m_gate) = split_cols(h @ lw['w_in'])
    xc, conv_new = causal_conv(rg_x, conv_buf, lw['rg_conv_w'], lw['rg_conv_b'])
    hr, h_last = rglru(xc, h0, lw['rg_w_a'], lw['rg_b_a'], lw['rg_w_x'], lw['rg_b_x'], lw['rg_lambda'])
    o_a = jax.nn.gelu(rg_y) * hr
    q = g_q.reshape(B, T, GLA_HEADS, GLA_DK) * (GLA_DK ** -0.5)
    k = g_k.reshape(B, T, GLA_HEADS, GLA_DK)
    v = g_v.reshape(B, T, GLA_HEADS, GLA_DV)
    log_g = jax.nn.log_sigmoid((g_a @ lw['gla_w_a'] + lw['gla_b_a']).astype(jnp.float32)) / GLA_TAU
    o, s_new = gla_recurrent(q, k, v, log_g.reshape(B, T, GLA_HEADS, GLA_DK), s0)
    o_b = rms_norm(o, lw['gla_norm']).reshape(B, T, GLA_HEADS * GLA_DV).astype(x.dtype) * jax.nn.silu(g_r)
    kv_shape = (B, T, 2, G, DH)
    qn = rms_norm(n_q.reshape(B, T, G, P, DH), lw['q_norm']) * (DH ** -0.5)
    cmp_kv = n_cmp.reshape(kv_shape)
    slc_kv = norm_keys(n_slc.reshape(kv_shape), lw['k_norm'][1])
    win_kv = norm_keys(n_win.reshape(kv_shape), lw['k_norm'][2])
    gates = jax.nn.sigmoid(n_gate).reshape(B, T, G, P, 3)
    o_c, win_state = nsa_fn(qn, cmp_kv, slc_kv, win_kv, gates, lw)
    m_a, m_b, m_c = jnp.split(jax.nn.sigmoid(m_gate), 3, axis=-1)
    merged = (m_a * (o_a @ lw['w_rg_out']) + m_b * (o_b @ lw['w_gla_out'])
              + m_c * (o_c @ lw['w_nsa_out']))
    x = x + merged @ lw['w_out']
    h = rms_norm(x, lw['norm_ffn'])
    x = x + (jax.nn.silu(h @ lw['w_ffn_gate']) * (h @ lw['w_ffn_up'])) @ lw['w_ffn_down']
    return x, (cmp_kv, slc_kv, win_state, s_new.astype(x.dtype), h_last.astype(x.dtype), conv_new)


def setup_inputs(seed: int = 0) -> dict:
    key = jax.random.key(seed)
    keys = jax.random.split(key, 40)
    counter = iter(range(40))
    f32 = jnp.float32

    def nrm(shape, scale):
        return jax.random.normal(keys[next(counter)], shape, f32) * scale

    G, DH = NSA_KV_HEADS, NSA_DH
    n_used = DEC_BATCH * N_PAGES
    n_pool = n_used + max(1, n_used // 4)
    win_buf = min(WINDOW, PAST_LEN)
    perm = jax.random.permutation(keys[next(counter)], n_pool)
    page_table = perm[:n_used].reshape(DEC_BATCH, N_PAGES).astype(jnp.int32)
    u = jax.random.uniform(keys[next(counter)], (DEPTH, RG_WIDTH), f32, 0.9, 0.999)
    sg = u ** (1.0 / RG_C)
    rg_lambda = jnp.log(sg) - jnp.log1p(-sg)
    return {
        'x_prompt': nrm((BATCH, SEQ, D_MODEL), 1.0),
        'x_sample': nrm((DEC_BATCH, DEC_SEQ, D_MODEL), 1.0),
        'cache_cmp_kv': nrm((DEPTH, n_pool, PAGE_SIZE, 2, G, DH), 1.0),
        'cache_slc_kv': nrm((DEPTH, n_pool, PAGE_SIZE, 2, G, DH), 1.0),
        'cache_win_kv': nrm((DEPTH, DEC_BATCH, win_buf, 2, G, DH), 1.0),
        'state_gla': nrm((DEPTH, DEC_BATCH, GLA_HEADS, GLA_DK, GLA_DV), 0.5),
        'state_rglru': nrm((DEPTH, DEC_BATCH, RG_WIDTH), 0.5),
        'state_conv': nrm((DEPTH, DEC_BATCH, RG_CONV_W - 1, RG_WIDTH), 1.0),
        'page_table': page_table,
        'norm_mix_g': 1.0 + nrm((DEPTH, D_MODEL), 0.02),
        'w_in': nrm((DEPTH, D_MODEL, D_IN), D_MODEL ** -0.5),
        'rg_conv_w': nrm((DEPTH, RG_CONV_W, RG_WIDTH), RG_CONV_W ** -0.5),
        'rg_conv_b': nrm((DEPTH, RG_WIDTH), 0.02),
        'rg_w_a': nrm((DEPTH, RG_BLOCKS, RG_BLOCK_W, RG_BLOCK_W), RG_BLOCK_W ** -0.5),
        'rg_b_a': nrm((DEPTH, RG_WIDTH), 0.02),
        'rg_w_x': nrm((DEPTH, RG_BLOCKS, RG_BLOCK_W, RG_BLOCK_W), RG_BLOCK_W ** -0.5),
        'rg_b_x': nrm((DEPTH, RG_WIDTH), 0.02),
        'rg_lambda': rg_lambda,
        'gla_w_a': nrm((DEPTH, GLA_RANK, GLA_HEADS * GLA_DK), GLA_RANK ** -0.5),
        'gla_b_a': nrm((DEPTH, GLA_HEADS * GLA_DK), 0.02),
        'gla_norm_g': 1.0 + nrm((DEPTH, GLA_DV), 0.02),
        'nsa_q_norm_g': 1.0 + nrm((DEPTH, DH), 0.02),
        'nsa_k_norm_g': 1.0 + nrm((DEPTH, 3, DH), 0.02),
        'nsa_phi_k': nrm((DEPTH, CMP_BLOCK, DH, DH), (CMP_BLOCK * DH) ** -0.5),
        'nsa_phi_v': nrm((DEPTH, CMP_BLOCK, DH, DH), (CMP_BLOCK * DH) ** -0.5),
        'nsa_cmp_pe': nrm((DEPTH, 2, CMP_BLOCK, DH), 0.1),
        'w_rg_out': nrm((DEPTH, RG_WIDTH, D_MODEL), RG_WIDTH ** -0.5),
        'w_gla_out': nrm((DEPTH, GLA_HEADS * GLA_DV, D_MODEL), (GLA_HEADS * GLA_DV) ** -0.5),
        'w_nsa_out': nrm((DEPTH, NSA_HEADS * DH, D_MODEL), (NSA_HEADS * DH) ** -0.5),
        'w_out': nrm((DEPTH, D_MODEL, D_MODEL), D_MODEL ** -0.5),
        'norm_ffn_g': 1.0 + nrm((DEPTH, D_MODEL), 0.02),
        'w_ffn_gate': nrm((DEPTH, D_MODEL, D_FF), D_MODEL ** -0.5),
        'w_ffn_up': nrm((DEPTH, D_MODEL, D_FF), D_MODEL ** -0.5),
        'w_ffn_down': nrm((DEPTH, D_FF, D_MODEL), D_FF ** -0.5),
    }


def reference(x_prompt, x_sample, cache_cmp_kv, cache_slc_kv, cache_win_kv, state_gla, state_rglru,
              state_conv, page_table, norm_mix_g, w_in, rg_conv_w, rg_conv_b, rg_w_a, rg_b_a, rg_w_x,
              rg_b_x, rg_lambda, gla_w_a, gla_b_a, gla_norm_g, nsa_q_norm_g, nsa_k_norm_g, nsa_phi_k,
              nsa_phi_v, nsa_cmp_pe, w_rg_out, w_gla_out, w_nsa_out, w_out, norm_ffn_g, w_ffn_gate,
              w_ffn_up, w_ffn_down):
    slopes = alibi_slopes()
    Bp = x_prompt.shape[0]
    yp, ys = x_prompt, x_sample
    p_out = [[] for _ in range(6)]
    s_out = [[] for _ in range(6)]
    for l in range(DEPTH):
        lw = {'norm_mix': norm_mix_g[l], 'w_in': w_in[l], 'rg_conv_w': rg_conv_w[l], 'rg_conv_b': rg_conv_b[l],
              'rg_w_a': rg_w_a[l], 'rg_b_a': rg_b_a[l], 'rg_w_x': rg_w_x[l], 'rg_b_x': rg_b_x[l],
              'rg_lambda': rg_lambda[l], 'gla_w_a': gla_w_a[l], 'gla_b_a': gla_b_a[l], 'gla_norm': gla_norm_g[l],
              'q_norm': nsa_q_norm_g[l], 'k_norm': nsa_k_norm_g[l], 'phi_k': nsa_phi_k[l], 'phi_v': nsa_phi_v[l],
              'cmp_pe': nsa_cmp_pe[l], 'w_rg_out': w_rg_out[l], 'w_gla_out': w_gla_out[l],
              'w_nsa_out': w_nsa_out[l], 'w_out': w_out[l], 'norm_ffn': norm_ffn_g[l],
              'w_ffn_gate': w_ffn_gate[l], 'w_ffn_up': w_ffn_up[l], 'w_ffn_down': w_ffn_down[l]}
        yp, p_new = layer_forward(
            yp, lw, functools.partial(nsa_prompt, slopes=slopes),
            jnp.zeros((Bp, RG_CONV_W - 1, RG_WIDTH), yp.dtype),
            jnp.zeros((Bp, RG_WIDTH), jnp.float32),
            jnp.zeros((Bp, GLA_HEADS, GLA_DK, GLA_DV), jnp.float32))
        ys, s_new = layer_forward(
            ys, lw, functools.partial(nsa_sample, slopes=slopes, cmp_pool=cache_cmp_kv[l],
                                      slc_pool=cache_slc_kv[l], win_buf=cache_win_kv[l], page_table=page_table),
            state_conv[l], state_rglru[l], state_gla[l])
        for i in range(6):
            p_out[i].append(p_new[i])
            s_out[i].append(s_new[i])
    p_cmp_kv = jnp.stack(p_out[0])
    p_slc_kv = jnp.stack(p_out[1])
    p_win_kv = jnp.stack(p_out[2])
    p_gla = jnp.stack(p_out[3])
    p_rglru = jnp.stack(p_out[4])
    p_conv = jnp.stack(p_out[5])
    s_cmp_kv = jnp.stack(s_out[0])
    s_slc_kv = jnp.stack(s_out[1])
    s_win_kv = jnp.stack(s_out[2])
    s_gla = jnp.stack(s_out[3])
    s_rglru = jnp.stack(s_out[4])
    s_conv = jnp.stack(s_out[5])
    return (yp, ys, p_cmp_kv, p_slc_kv, p_win_kv, p_gla, p_rglru, p_conv,
            s_cmp_kv, s_slc_kv, s_win_kv, s_gla, s_rglru, s_conv)
```

```python
import functools

import jax
import jax.numpy as jnp
from jax import lax
from jax.experimental import pallas as pl
from jax.experimental.pallas import tpu as pltpu

F32 = jnp.float32
BF16 = jnp.bfloat16

D_MODEL = 1024
PAGE_SIZE = 128
RG_BLOCKS = 8
RG_BLOCK_W = D_MODEL // RG_BLOCKS
RG_CONV_W = 4
RG_C = 8.0
GLA_HEADS = 4
GLA_DK = 128
GLA_DV = 256
GLA_RANK = 16
GLA_TAU = 16.0
GLA_CHUNK = 64
GLA_SUB = 16
NSA_HEADS = 16
NSA_G = 4
NSA_P = 4
NSA_DH = 64
CMP_BLOCK = 32
CMP_STRIDE = 16
SEL_BLOCK = 64
TOP_N = 16
WINDOW = 512
D_FF = 2816
KV_W = 512
EPS = 1e-6
NEG = -1e30
TINY = 1e-30
FORCE_SCORE = 1e4

COL_TILE = 512
C_MGATE = 0
C_RGY = 3072
C_RGX = 4096
C_GQ = 5120
C_GK = 5632
C_GV = 6144
C_GR = 7168
C_NQ = 8192
C_NCMP = 9216
C_NSLC = 9728
C_NWIN = 10240
C_SMALL = 10752
ZC = 11264
SMALL_W = 128
GATE_OFF = GLA_RANK

VMEM_LIMIT = 56 * 1024 * 1024


def _cp(sem, vmem=VMEM_LIMIT):
    return pltpu.CompilerParams(dimension_semantics=sem, vmem_limit_bytes=vmem)


def _nt(a, b):
    return lax.dot_general(a, b, (((1,), (1,)), ((), ())), preferred_element_type=F32)


def _tn(a, b):
    return lax.dot_general(a, b, (((0,), (0,)), ((), ())), preferred_element_type=F32)


def _dot(a, b):
    return jnp.dot(a, b, preferred_element_type=F32)


def _softplus(x):
    return jnp.maximum(x, 0.0) + jnp.log1p(jnp.exp(-jnp.abs(x)))


def _log_sigmoid(x):
    return jnp.minimum(x, 0.0) - jnp.log1p(jnp.exp(-jnp.abs(x)))


def _gelu_tanh(x):
    c = 0.7978845608028654
    return x * (0.5 * (1.0 + jnp.tanh(c * (x + 0.044715 * (x * x * x)))))


def _silu(x):
    return x * jax.nn.sigmoid(x)


def _seg_mean_sq(x, seg_ones):
    x2 = x * x
    hi = x2.astype(BF16)
    lo = (x2 - hi.astype(F32)).astype(BF16)
    outs = []
    for c in range(x.shape[1] // 128):
        sl = slice(c * 128, (c + 1) * 128)
        outs.append(_dot(hi[:, sl], seg_ones) + _dot(lo[:, sl], seg_ones))
    y = outs[0] if len(outs) == 1 else jnp.concatenate(outs, axis=1)
    return y * (1.0 / NSA_DH)


def _inproj_kernel(x_ref, g_ref, w_ref, o_ref, h_scr):
    @pl.when(pl.program_id(1) == 0)
    def _():
        x = x_ref[...]
        ms = jnp.mean(x * x, axis=-1, keepdims=True)
        h_scr[...] = (x * lax.rsqrt(ms + EPS) * g_ref[...]).astype(BF16)

    o_ref[...] = _dot(h_scr[...], w_ref[...])


def _inproj(x2d, g, w_perm):
    n = x2d.shape[0]
    tm = min(n, 1024)
    return pl.pallas_call(
        _inproj_kernel,
        grid=(n // tm, ZC // COL_TILE),
        in_specs=[
            pl.BlockSpec((tm, D_MODEL), lambda i, j: (i, 0)),
            pl.BlockSpec((1, D_MODEL), lambda i, j: (0, 0)),
            pl.BlockSpec((D_MODEL, COL_TILE), lambda i, j: (0, j)),
        ],
        out_specs=pl.BlockSpec((tm, COL_TILE), lambda i, j: (i, j)),
        out_shape=jax.ShapeDtypeStruct((n, ZC), F32),
        scratch_shapes=[pltpu.VMEM((tm, D_MODEL), BF16)],
        compiler_params=_cp(("arbitrary", "arbitrary")),
        name="inproj",
    )(x2d, g, w_perm)


def _rglru_gates(xc, wax_ref, ba, bx, lam):
    sp = _softplus(-lam)
    a_parts, u_parts = [], []
    for n in range(RG_BLOCKS):
        sl = slice(n * RG_BLOCK_W, (n + 1) * RG_BLOCK_W)
        xb = xc[:, sl]
        r = _dot(xb.astype(BF16), wax_ref[n])
        rec = jax.nn.sigmoid(r[:, :RG_BLOCK_W] + ba[:, sl])
        inp = jax.nn.sigmoid(r[:, RG_BLOCK_W:] + bx[:, sl])
        log_a = (-RG_C) * rec * sp[:, sl]
        a = jnp.exp(log_a)
        u = jnp.sqrt(-(jnp.tanh(log_a) * (a * a + 1.0))) * (inp * xb)
        a_parts.append(a)
        u_parts.append(u)
    return a_parts, u_parts


def _rglru_kernel(y_ref, x_ref, cw_ref, cb_ref, wax_ref, ba_ref, bx_ref, lam_ref,
                  o_ref, hl_ref, xp_scr, a_scr, u_scr, h_scr):
    t = pl.program_id(1)
    tt = x_ref.shape[1]

    @pl.when(t == 0)
    def _():
        xp_scr[0:8, :] = jnp.zeros((8, D_MODEL), F32)
        h_scr[...] = jnp.zeros_like(h_scr)

    x = x_ref[0]
    xp_scr[8:8 + tt, :] = x
    cw = cw_ref[...]
    xc = (cb_ref[...] + xp_scr[5:5 + tt, :] * cw[0:1] + xp_scr[6:6 + tt, :] * cw[1:2]
          + xp_scr[7:7 + tt, :] * cw[2:3] + x * cw[3:4])
    xp_scr[0:8, :] = xp_scr[tt:tt + 8, :]

    a_parts, u_parts = _rglru_gates(xc, wax_ref, ba_ref[...], bx_ref[...], lam_ref[...])
    for n in range(RG_BLOCKS):
        sl = slice(n * RG_BLOCK_W, (n + 1) * RG_BLOCK_W)
        a_scr[:, sl] = a_parts[n]
        u_scr[:, sl] = u_parts[n]

    row = lax.broadcasted_iota(jnp.int32, (8, D_MODEL), 0)

    def body(i, h):
        r0 = pl.multiple_of(i * 8, 8)
        a = a_scr[pl.ds(r0, 8), :]
        u = u_scr[pl.ds(r0, 8), :]
        for d in (1, 2, 4):
            a_sh = pltpu.roll(a, d, 0)
            u_sh = pltpu.roll(u, d, 0)
            m = row >= d
            u = jnp.where(m, a * u_sh + u, u)
            a = jnp.where(m, a * a_sh, a)
        hh = u + a * h
        u_scr[pl.ds(r0, 8), :] = hh
        return hh[7:8, :]

    h = lax.fori_loop(0, tt // 8, body, h_scr[0:1, :])
    h_scr[...] = jnp.broadcast_to(h, h_scr.shape)
    hl_ref[0] = jnp.broadcast_to(h, (8, D_MODEL))
    o_ref[0] = _gelu_tanh(y_ref[0]) * u_scr[...]


def _rglru_prompt(z3, cw, cb, wax, ba, bx, lam):
    b, t, _ = z3.shape
    tt = min(t, 512)
    row = lambda i, j: (0, 0)
    return pl.pallas_call(
        _rglru_kernel,
        grid=(b, t // tt),
        in_specs=[
            pl.BlockSpec((1, tt, D_MODEL), lambda i, j: (i, j, C_RGY // D_MODEL)),
            pl.BlockSpec((1, tt, D_MODEL), lambda i, j: (i, j, C_RGX // D_MODEL)),
            pl.BlockSpec((RG_CONV_W, D_MODEL), row),
            pl.BlockSpec((1, D_MODEL), row),
            pl.BlockSpec((RG_BLOCKS, RG_BLOCK_W, 2 * RG_BLOCK_W), lambda i, j: (0, 0, 0)),
            pl.BlockSpec((1, D_MODEL), row),
            pl.BlockSpec((1, D_MODEL), row),
            pl.BlockSpec((1, D_MODEL), row),
        ],
        out_specs=[
            pl.BlockSpec((1, tt, D_MODEL), lambda i, j: (i, j, 0)),
            pl.BlockSpec((1, 8, D_MODEL), lambda i, j: (i, 0, 0)),
        ],
        out_shape=[
            jax.ShapeDtypeStruct((b, t, D_MODEL), F32),
            jax.ShapeDtypeStruct((b, 8, D_MODEL), F32),
        ],
        scratch_shapes=[
            pltpu.VMEM((tt + 8, D_MODEL), F32),
            pltpu.VMEM((tt, D_MODEL), F32),
            pltpu.VMEM((tt, D_MODEL), F32),
            pltpu.VMEM((8, D_MODEL), F32),
        ],
        compiler_params=_cp(("arbitrary", "arbitrary")),
        name="rglru_prompt",
    )(z3, z3, cw, cb, wax, ba, bx, lam)


def _gla_chunk_head(q, k, v, lg, st):
    c = GLA_CHUNK
    r_i = lax.broadcasted_iota(jnp.int32, (c, c), 0)
    c_i = lax.broadcasted_iota(jnp.int32, (c, c), 1)
    tril = (r_i >= c_i).astype(F32)
    b = jnp.dot(tril, lg, preferred_element_type=F32, precision=lax.Precision.HIGHEST)
    kb = k.astype(BF16)
    vb = v.astype(BF16)
    o_state = _nt((q * jnp.exp(b)).astype(BF16), st.astype(BF16))
    sub = GLA_SUB
    t_row = lax.broadcasted_iota(jnp.int32, (sub, GLA_DK), 0)
    lane = lax.broadcasted_iota(jnp.int32, (sub, GLA_DK), 1)
    o_parts = []
    for blk in range(c // sub):
        r0 = blk * sub
        q_i = q[r0:r0 + sub]
        k_i = k[r0:r0 + sub]
        b_i = b[r0:r0 + sub]
        a_d = jnp.zeros((sub, GLA_DK), F32)
        for s in range(sub):
            diff = b_i - b_i[s:s + 1]
            dec = jnp.exp(jnp.where(t_row >= s, diff, NEG))
            col = jnp.sum(q_i * k_i[s:s + 1] * dec, axis=-1, keepdims=True)
            a_d = jnp.where(lane == s, col, a_d)
        o_i = _dot(a_d[:, :sub].astype(BF16), vb[r0:r0 + sub])
        if blk > 0:
            b_ref = b[r0 - 1:r0]
            qd = (q_i * jnp.exp(b_i - b_ref)).astype(BF16)
            kd = (k[:r0] * jnp.exp(b_ref - b[:r0])).astype(BF16)
            a_off = _nt(qd, kd)
            o_i = o_i + _dot(a_off.astype(BF16), vb[:r0])
        o_parts.append(o_i)
    o = o_state + jnp.concatenate(o_parts, axis=0)
    b_last = b[c - 1:c]
    kd = (k * jnp.exp(b_last - b)).astype(BF16)
    st_new = st * jnp.exp(b_last) + _tn(vb, kd)
    del kb
    return o, st_new


def _gla_kernel(q_ref, k_ref, v_ref, r_ref, ga_ref, wa_ref, ba_ref, gn_ref,
                o_ref, st_ref, st_scr):
    t = pl.program_id(1)
    tt = q_ref.shape[1]

    @pl.when(t == 0)
    def _():
        st_scr[...] = jnp.zeros_like(st_scr)

    gn = gn_ref[...]

    def body(ci, carry):
        r0 = pl.multiple_of(ci * GLA_CHUNK, GLA_CHUNK)
        rows = pl.ds(r0, GLA_CHUNK)
        xg = _dot(ga_ref[0, rows, :].astype(BF16), wa_ref[...]) + ba_ref[...]
        lg_all = _log_sigmoid(xg) * (1.0 / GLA_TAU)
        for h in range(GLA_HEADS):
            ksl = slice(h * GLA_DK, (h + 1) * GLA_DK)
            vsl = slice(h * GLA_DV, (h + 1) * GLA_DV)
            q = q_ref[0, rows, ksl] * (GLA_DK ** -0.5)
            k = k_ref[0, rows, ksl]
            v = v_ref[0, rows, vsl]
            o, st_new = _gla_chunk_head(q, k, v, lg_all[:, ksl], st_scr[h])
            st_scr[h] = st_new
            ms = jnp.mean(o * o, axis=-1, keepdims=True)
            on = o * lax.rsqrt(ms + EPS) * gn
            o_ref[0, rows, vsl] = on * _silu(r_ref[0, rows, vsl])
        return carry

    lax.fori_loop(0, tt // GLA_CHUNK, body, 0)
    st_ref[0] = st_scr[...]


def _gla_prompt(z3, wa_pad, ba, gn):
    b, t, _ = z3.shape
    tt = min(t, 256)
    hk = GLA_HEADS * GLA_DK
    hv = GLA_HEADS * GLA_DV
    row = lambda i, j: (0, 0)
    return pl.pallas_call(
        _gla_kernel,
        grid=(b, t // tt),
        in_specs=[
            pl.BlockSpec((1, tt, hk), lambda i, j: (i, j, C_GQ // hk)),
            pl.BlockSpec((1, tt, hk), lambda i, j: (i, j, C_GK // hk)),
            pl.BlockSpec((1, tt, hv), lambda i, j: (i, j, C_GV // hv)),
            pl.BlockSpec((1, tt, hv), lambda i, j: (i, j, C_GR // hv)),
            pl.BlockSpec((1, tt, SMALL_W), lambda i, j: (i, j, C_SMALL // SMALL_W)),
            pl.BlockSpec((SMALL_W, hk), row),
            pl.BlockSpec((1, hk), row),
            pl.BlockSpec((1, GLA_DV), row),
        ],
        out_specs=[
            pl.BlockSpec((1, tt, hv), lambda i, j: (i, j, 0)),
            pl.BlockSpec((1, GLA_HEADS, GLA_DV, GLA_DK), lambda i, j: (i, 0, 0, 0)),
        ],
        out_shape=[
            jax.ShapeDtypeStruct((b, t, hv), F32),
            jax.ShapeDtypeStruct((b, GLA_HEADS, GLA_DV, GLA_DK), F32),
        ],
        scratch_shapes=[pltpu.VMEM((GLA_HEADS, GLA_DV, GLA_DK), F32)],
        compiler_params=_cp(("arbitrary", "arbitrary")),
        name="gla_prompt",
    )(z3, z3, z3, z3, z3, wa_pad, ba, gn)


def _nsa_prep_kernel(nq_ref, slc_ref, win_ref, sm_ref, qg_ref, kg_ref, seg_ref,
                     q_ref, slc_o, ks_ref, vs_ref, win_o, kw_ref, vw_ref, gt_ref):
    seg = seg_ref[...]
    half = NSA_G * NSA_DH
    x = nq_ref[0]
    qn = x * lax.rsqrt(_seg_mean_sq(x, seg) + EPS) * qg_ref[...] * (NSA_DH ** -0.5)
    for h in range(NSA_HEADS):
        q_ref[0, h] = qn[:, h * NSA_DH:(h + 1) * NSA_DH].astype(BF16)
    for idx, (src, dst, kh, vh) in enumerate(((slc_ref, slc_o, ks_ref, vs_ref),
                                              (win_ref, win_o, kw_ref, vw_ref))):
        kv = src[0]
        kk = kv[:, :half]
        vv = kv[:, half:]
        kn = kk * lax.rsqrt(_seg_mean_sq(kk, seg) + EPS) * kg_ref[idx + 1:idx + 2, :]
        dst[0, :, :half] = kn
        dst[0, :, half:] = vv
        for g in range(NSA_G):
            sl = slice(g * NSA_DH, (g + 1) * NSA_DH)
            kh[0, g] = kn[:, sl].astype(BF16)
            vh[0, g] = vv[:, sl].astype(BF16)
    sig = jax.nn.sigmoid(sm_ref[0])
    for g in range(NSA_G):
        lo = GATE_OFF + g * NSA_P * 3
        gt_ref[0, g] = sig[:, lo:lo + NSA_P * 3]


def _nsa_prep(z3, qg_t, kg_t, seg_ones):
    b, t, _ = z3.shape
    tt = min(t, 512)
    row = lambda i, j: (0, 0)
    hm_q = pl.BlockSpec((1, NSA_HEADS, tt, NSA_DH), lambda i, j: (i, 0, j, 0))
    hm_kv = pl.BlockSpec((1, NSA_G, tt, NSA_DH), lambda i, j: (i, 0, j, 0))
    flat = pl.BlockSpec((1, tt, KV_W), lambda i, j: (i, j, 0))
    sd = jax.ShapeDtypeStruct
    return pl.pallas_call(
        _nsa_prep_kernel,
        grid=(b, t // tt),
        in_specs=[
            pl.BlockSpec((1, tt, D_MODEL), lambda i, j: (i, j, C_NQ // D_MODEL)),
            pl.BlockSpec((1, tt, KV_W), lambda i, j: (i, j, C_NSLC // KV_W)),
            pl.BlockSpec((1, tt, KV_W), lambda i, j: (i, j, C_NWIN // KV_W)),
            pl.BlockSpec((1, tt, SMALL_W), lambda i, j: (i, j, C_SMALL // SMALL_W)),
            pl.BlockSpec((1, D_MODEL), row),
            pl.BlockSpec((3, NSA_G * NSA_DH), row),
            pl.BlockSpec((128, 128), row),
        ],
        out_specs=[hm_q, flat, hm_kv, hm_kv, flat, hm_kv, hm_kv,
                   pl.BlockSpec((1, NSA_G, tt, NSA_P * 3), lambda i, j: (i, 0, j, 0))],
        out_shape=[
            sd((b, NSA_HEADS, t, NSA_DH), BF16),
            sd((b, t, KV_W), F32),
            sd((b, NSA_G, t, NSA_DH), BF16),
            sd((b, NSA_G, t, NSA_DH), BF16),
            sd((b, t, KV_W), F32),
            sd((b, NSA_G, t, NSA_DH), BF16),
            sd((b, NSA_G, t, NSA_DH), BF16),
            sd((b, NSA_G, t, NSA_P * 3), F32),
        ],
        compiler_params=_cp(("arbitrary", "arbitrary")),
        name="nsa_prep",
    )(z3, z3, z3, z3, qg_t, kg_t, seg_ones)


def _compress_rows(load_rows, nch, phk_ref, phv_ref, pek_ref, pev_ref, sh_scr):
    half = NSA_G * NSA_DH
    acc = [jnp.zeros((nch, half), F32) for _ in range(4)]
    for s in range(CMP_STRIDE):
        x = load_rows(s)
        xk = x[:, :half]
        xv = x[:, half:]
        acc[0] += _dot((xk + pek_ref[s:s + 1, :]).astype(BF16), phk_ref[s])
        acc[1] += _dot((xk + pek_ref[CMP_STRIDE + s:CMP_STRIDE + s + 1, :]).astype(BF16),
                       phk_ref[CMP_STRIDE + s])
        acc[2] += _dot((xv + pev_ref[s:s + 1, :]).astype(BF16), phv_ref[s])
        acc[3] += _dot((xv + pev_ref[CMP_STRIDE + s:CMP_STRIDE + s + 1, :]).astype(BF16),
                       phv_ref[CMP_STRIDE + s])
    outs = []
    for lo, hi in ((acc[0], acc[1]), (acc[2], acc[3])):
        sh_scr[0:nch, :] = hi
        sh_scr[nch:nch + 8, :] = jnp.zeros((8, half), F32)
        outs.append(lo + sh_scr[1:nch + 1, :])
    return outs[0], outs[1]


def _compress_kernel(x_ref, phk_ref, phv_ref, pek_ref, pev_ref, kg_ref, seg_ref,
                     kc_ref, vc_ref, sh_scr):
    nch = x_ref.shape[1]
    k, v = _compress_rows(lambda s: x_ref[0, :, s * KV_W:(s + 1) * KV_W], nch,
                          phk_ref, phv_ref, pek_ref, pev_ref, sh_scr)
    kn = k * lax.rsqrt(_seg_mean_sq(k, seg_ref[...]) + EPS) * kg_ref[0:1, :]
    for g in range(NSA_G):
        sl = slice(g * NSA_DH, (g + 1) * NSA_DH)
        kc_ref[0, g] = kn[:, sl].astype(BF16)
        vc_ref[0, g] = v[:, sl].astype(BF16)


def _compress_prompt(cmp_rows, phk, phv, pek, pev, kg_t, seg_ones):
    b, nch, _ = cmp_rows.shape
    half = NSA_G * NSA_DH
    c3 = lambda i: (0, 0, 0)
    c2 = lambda i: (0, 0)
    hm = pl.BlockSpec((1, NSA_G, nch, NSA_DH), lambda i: (i, 0, 0, 0))
    return pl.pallas_call(
        _compress_kernel,
        grid=(b,),
        in_specs=[
            pl.BlockSpec((1, nch, CMP_STRIDE * KV_W), lambda i: (i, 0, 0)),
            pl.BlockSpec((CMP_BLOCK, half, half), c3),
            pl.BlockSpec((CMP_BLOCK, half, half), c3),
            pl.BlockSpec((CMP_BLOCK, half), c2),
            pl.BlockSpec((CMP_BLOCK, half), c2),
            pl.BlockSpec((3, half), c2),
            pl.BlockSpec((128, 128), c2),
        ],
        out_specs=[hm, hm],
        out_shape=[jax.ShapeDtypeStruct((b, NSA_G, nch, NSA_DH), BF16)] * 2,
        scratch_shapes=[pltpu.VMEM((nch + 8, half), F32)],
        compiler_params=_cp(("arbitrary",)),
        name="compress_prompt",
    )(cmp_rows, phk, phv, pek, pev, kg_t, seg_ones)


def _gated_heads(o, gates, c):
    qb = o.shape[0] // NSA_P
    parts = []
    for p in range(NSA_P):
        gcol = gates[:, p * 3 + c:p * 3 + c + 1]
        parts.append(o[p * qb:(p + 1) * qb] * gcol)
    return jnp.concatenate(parts, axis=1)


def _rank_select(score_t, n_keep):
    nj = score_t.shape[0]
    jrow = lax.broadcasted_iota(jnp.int32, score_t.shape, 0)
    rank = jnp.zeros(score_t.shape, F32)
    for jp in range(nj):
        r = score_t[jp:jp + 1, :]
        rank = rank + jnp.where(jrow > jp, jnp.where(r >= score_t, 1.0, 0.0),
                                jnp.where(r > score_t, 1.0, 0.0))
    return jnp.where(rank < n_keep, 1.0, 0.0)


def _cmp_attn_kernel(q_ref, kc_ref, vc_ref, gt_ref, sl_ref, m2s_ref,
                     o_ref, sel_ref, *, n_keep):
    i = pl.program_id(2)
    qb = q_ref.shape[2]
    rows = NSA_P * qb
    nc = kc_ref.shape[2]
    q = q_ref[0].reshape(rows, NSA_DH)
    s = _nt(q, kc_ref[0, 0])
    r_i = lax.broadcasted_iota(jnp.int32, (rows, nc), 0)
    q_pos = i * qb + r_i % qb
    c_end = lax.broadcasted_iota(jnp.int32, (rows, nc), 1) * CMP_STRIDE + (CMP_BLOCK - 1)
    mask = c_end <= q_pos
    dist = (q_pos - c_end).astype(F32)
    sc = jnp.where(mask, s - sl_ref[0] * dist, NEG)
    m = jnp.max(sc, axis=-1, keepdims=True)
    e = jnp.where(mask, jnp.exp(sc - m), 0.0)
    p = e / jnp.maximum(jnp.sum(e, axis=-1, keepdims=True), TINY)
    o = _dot(p.astype(BF16), vc_ref[0, 0])
    o_ref[0] = _gated_heads(o, gt_ref[0, 0], 0)
    ps = p[0:qb]
    for pp in range(1, NSA_P):
        ps = ps + p[pp * qb:(pp + 1) * qb]
    hi = ps.astype(BF16)
    lo = (ps - hi.astype(F32)).astype(BF16)
    imp_t = _nt(m2s_ref[...], hi) + _nt(m2s_ref[...], lo)
    nj = imp_t.shape[0]
    jrow = lax.broadcasted_iota(jnp.int32, (nj, qb), 0)
    valid = jrow <= i
    forced = (jrow == 0) | (jrow == i) | (jrow == i - 1)
    score_t = jnp.where(valid, jnp.where(forced, FORCE_SCORE, imp_t), NEG)
    sel_t = _rank_select(score_t, n_keep)
    eye = (lax.broadcasted_iota(jnp.int32, (qb, qb), 0)
           == lax.broadcasted_iota(jnp.int32, (qb, qb), 1)).astype(BF16)
    sel_ref[0, 0] = _nt(eye, sel_t.astype(BF16))


def _cmp_attn_prompt(q_hm, kc, vc, gates, slope_col, m2s_t):
    b, _, t, _ = q_hm.shape
    qb = SEL_BLOCK
    nqb = t // qb
    nc = kc.shape[2]
    nsel = m2s_t.shape[0]
    return pl.pallas_call(
        functools.partial(_cmp_attn_kernel, n_keep=min(TOP_N, nsel)),
        grid=(b, NSA_G, nqb),
        in_specs=[
            pl.BlockSpec((1, NSA_P, qb, NSA_DH), lambda bi, g, i: (bi, g, i, 0)),
            pl.BlockSpec((1, 1, nc, NSA_DH), lambda bi, g, i: (bi, g, 0, 0)),
            pl.BlockSpec((1, 1, nc, NSA_DH), lambda bi, g, i: (bi, g, 0, 0)),
            pl.BlockSpec((1, 1, qb, NSA_P * 3), lambda bi, g, i: (bi, g, i, 0)),
            pl.BlockSpec((1, NSA_P * qb, 1), lambda bi, g, i: (g, 0, 0)),
            pl.BlockSpec((nsel, nc), lambda bi, g, i: (0, 0)),
        ],
        out_specs=[
            pl.BlockSpec((1, qb, NSA_P * NSA_DH), lambda bi, g, i: (bi, i, g)),
            pl.BlockSpec((1, 1, qb, nsel), lambda bi, g, i: (bi, g, i, 0)),
        ],
        out_shape=[
            jax.ShapeDtypeStruct((b, t, D_MODEL), F32),
            jax.ShapeDtypeStruct((b, NSA_G, t, nsel), F32),
        ],
        compiler_params=_cp(("arbitrary", "arbitrary", "arbitrary")),
        name="cmp_attn_prompt",
    )(q_hm, kc, vc, gates, slope_col, m2s_t)


KEY_TILE = 128


def _tile_attn_kernel(q_ref, k_ref, v_ref, gt_ref, sl_ref, sel_ref, ex_ref, o_ref, *, mode):
    i = pl.program_id(2)
    qb = q_ref.shape[2]
    rows = NSA_P * qb
    q = q_ref[0].reshape(rows, NSA_DH)
    slope = sl_ref[0]
    q_pos = i * qb + lax.broadcasted_iota(jnp.int32, (rows, KEY_TILE), 0) % qb
    lane = lax.broadcasted_iota(jnp.int32, (rows, KEY_TILE), 1)
    if mode == "slc":
        sel_b = sel_ref[0, 0].astype(BF16)
        lo_tile = 0
    else:
        lo_tile = jnp.maximum(i * qb - WINDOW, 0) // KEY_TILE
    hi_tile = (i * qb) // KEY_TILE + 1

    def body(kt, carry):
        m, l, acc = carry
        k0 = pl.multiple_of(kt * KEY_TILE, KEY_TILE)
        kk = k_ref[0, 0, pl.ds(k0, KEY_TILE), :]
        vv = v_ref[0, 0, pl.ds(k0, KEY_TILE), :]
        s = _nt(q, kk)
        dist = q_pos - (k0 + lane)
        if mode == "slc":
            sel_k = _dot(sel_b, ex_ref[kt])
            sel_k = jnp.concatenate([sel_k] * NSA_P, axis=0)
            mask = (sel_k > 0.5) & (dist >= 0)
        else:
            mask = (dist >= 0) & (dist <= WINDOW)
        sc = jnp.where(mask, s - slope * dist.astype(F32), NEG)
        m_new = jnp.maximum(m, jnp.max(sc, axis=-1, keepdims=True))
        alpha = jnp.exp(m - m_new)
        e = jnp.where(mask, jnp.exp(sc - m_new), 0.0)
        l = alpha * l + jnp.sum(e, axis=-1, keepdims=True)
        acc = alpha * acc + _dot(e.astype(BF16), vv)
        return m_new, l, acc

    init = (jnp.full((rows, 1), NEG, F32), jnp.zeros((rows, 1), F32),
            jnp.zeros((rows, NSA_DH), F32))
    _, l, acc = lax.fori_loop(lo_tile, hi_tile, body, init)
    o = acc / jnp.maximum(l, TINY)
    o_ref[0] = _gated_heads(o, gt_ref[0, 0], 1 if mode == "slc" else 2)


def _tile_attn_prompt(q_hm, k_hm, v_hm, gates, slope_col, sel, expand, mode):
    b, _, t, _ = q_hm.shape
    qb = SEL_BLOCK
    nqb = t // qb
    nsel = sel.shape[-1]
    n_kt = expand.shape[0]
    return pl.pallas_call(
        functools.partial(_tile_attn_kernel, mode=mode),
        grid=(b, NSA_G, nqb),
        in_specs=[
            pl.BlockSpec((1, NSA_P, qb, NSA_DH), lambda bi, g, i: (bi, g, i, 0)),
            pl.BlockSpec((1, 1, t, NSA_DH), lambda bi, g, i: (bi, g, 0, 0)),
            pl.BlockSpec((1, 1, t, NSA_DH), lambda bi, g, i: (bi, g, 0, 0)),
            pl.BlockSpec((1, 1, qb, NSA_P * 3), lambda bi, g, i: (bi, g, i, 0)),
            pl.BlockSpec((1, NSA_P * qb, 1), lambda bi, g, i: (g, 0, 0)),
            pl.BlockSpec((1, 1, qb, nsel), lambda bi, g, i: (bi, g, i, 0)),
            pl.BlockSpec((n_kt, nsel, KEY_TILE), lambda bi, g, i: (0, 0, 0)),
        ],
        out_specs=pl.BlockSpec((1, qb, NSA_P * NSA_DH), lambda bi, g, i: (bi, i, g)),
        out_shape=jax.ShapeDtypeStruct((b, t, D_MODEL), F32),
        compiler_params=_cp(("arbitrary", "arbitrary", "arbitrary")),
        name="attn_" + mode,
    )(q_hm, k_hm, v_hm, gates, slope_col, sel, expand)


def _merge_kernel(x_ref, oa_ref, ob_ref, c1_ref, c2_ref, c3_ref, ma_ref, mb_ref, mc_ref,
                  wa_ref, wb_ref, wc_ref, wo_ref, o_ref):
    oc = c1_ref[...] + c2_ref[...] + c3_ref[...]
    merged = (jax.nn.sigmoid(ma_ref[...]) * _dot(oa_ref[...].astype(BF16), wa_ref[...])
              + jax.nn.sigmoid(mb_ref[...]) * _dot(ob_ref[...].astype(BF16), wb_ref[...])
              + jax.nn.sigmoid(mc_ref[...]) * _dot(oc.astype(BF16), wc_ref[...]))
    o_ref[...] = x_ref[...] + _dot(merged.astype(BF16), wo_ref[...])


def _merge(x2d, z2d, oa, ob, c1, c2, c3, wa, wb, wc, wo):
    n = x2d.shape[0]
    tt = min(n, 256)
    tok = pl.BlockSpec((tt, D_MODEL), lambda i: (i, 0))
    wsp = pl.BlockSpec((D_MODEL, D_MODEL), lambda i: (0, 0))
    zsp = [pl.BlockSpec((tt, D_MODEL), functools.partial(lambda i, c: (i, c), c=C_MGATE // D_MODEL + c))
           for c in range(3)]
    return pl.pallas_call(
        _merge_kernel,
        grid=(n // tt,),
        in_specs=[tok] * 6 + zsp + [wsp] * 4,
        out_specs=tok,
        out_shape=jax.ShapeDtypeStruct((n, D_MODEL), F32),
        compiler_params=_cp(("arbitrary",)),
        name="merge",
    )(x2d, oa, ob, c1, c2, c3, z2d, z2d, z2d, wa, wb, wc, wo)


FF_TILE = 1408


def _ffn_kernel(x_ref, g_ref, wg_ref, wu_ref, wd_ref, o_ref, h_scr, acc_scr):
    j = pl.program_id(1)

    @pl.when(j == 0)
    def _():
        x = x_ref[...]
        ms = jnp.mean(x * x, axis=-1, keepdims=True)
        h_scr[...] = (x * lax.rsqrt(ms + EPS) * g_ref[...]).astype(BF16)
        acc_scr[...] = x

    h = h_scr[...]
    act = _silu(_dot(h, wg_ref[...])) * _dot(h, wu_ref[...])
    acc_scr[...] += _dot(act.astype(BF16), wd_ref[...])

    @pl.when(j == pl.num_programs(1) - 1)
    def _():
        o_ref[...] = acc_scr[...]


def _ffn(x2d, g, wg, wu, wd):
    n = x2d.shape[0]
    tt = min(n, 512)
    return pl.pallas_call(
        _ffn_kernel,
        grid=(n // tt, D_FF // FF_TILE),
        in_specs=[
            pl.BlockSpec((tt, D_MODEL), lambda i, j: (i, 0)),
            pl.BlockSpec((1, D_MODEL), lambda i, j: (0, 0)),
            pl.BlockSpec((D_MODEL, FF_TILE), lambda i, j: (0, j)),
            pl.BlockSpec((D_MODEL, FF_TILE), lambda i, j: (0, j)),
            pl.BlockSpec((FF_TILE, D_MODEL), lambda i, j: (j, 0)),
        ],
        out_specs=pl.BlockSpec((tt, D_MODEL), lambda i, j: (i, 0)),
        out_shape=jax.ShapeDtypeStruct((n, D_MODEL), F32),
        scratch_shapes=[pltpu.VMEM((tt, D_MODEL), BF16), pltpu.VMEM((tt, D_MODEL), F32)],
        compiler_params=_cp(("arbitrary", "arbitrary")),
        name="ffn",
    )(x2d, g, wg, wu, wd)


IN_SIZES = (1024, 1024, 512, 512, 1024, 1024, 16, 1024, 512, 512, 512, 48, 3072)


def _perm_w_in(w):
    parts, start = [], 0
    for n in IN_SIZES:
        parts.append(w[:, start:start + n])
        start += n
    (rg_y, rg_x, g_q, g_k, g_v, g_r, g_a, n_q, n_cmp, n_slc, n_win, n_gate, m_gate) = parts
    pad = jnp.zeros((w.shape[0], ZC - C_SMALL - GLA_RANK - 3 * NSA_HEADS), w.dtype)
    return jnp.concatenate([m_gate, rg_y, rg_x, g_q, g_k, g_v, g_r, n_q, n_cmp, n_slc, n_win,
                            g_a, n_gate, pad], axis=1).astype(BF16)


def _block_diag4(phi):
    eye = jnp.eye(NSA_G, dtype=phi.dtype)
    out = jnp.einsum('gh,sde->sgdhe', eye, phi)
    return out.reshape(phi.shape[0], NSA_G * NSA_DH, NSA_G * NSA_DH)


def _alibi_slopes():
    h = jnp.arange(1, NSA_HEADS + 1, dtype=F32)
    return (2.0 ** (-8.0 * h / NSA_HEADS)).reshape(NSA_G, NSA_P)


def _layer_params(l, norm_mix_g, w_in, rg_conv_w, rg_conv_b, rg_w_a, rg_b_a, rg_w_x, rg_b_x,
                  rg_lambda, gla_w_a, gla_b_a, gla_norm_g, nsa_q_norm_g, nsa_k_norm_g, nsa_phi_k,
                  nsa_phi_v, nsa_cmp_pe, w_rg_out, w_gla_out, w_nsa_out, w_out, norm_ffn_g,
                  w_ffn_gate, w_ffn_up, w_ffn_down):
    half = NSA_G * NSA_DH
    p = {}
    p['norm_mix'] = norm_mix_g[l][None, :]
    p['w_in'] = _perm_w_in(w_in[l])
    p['cw'] = rg_conv_w[l]
    p['cb'] = rg_conv_b[l][None, :]
    p['wax'] = jnp.concatenate([rg_w_a[l], rg_w_x[l]], axis=-1).astype(BF16)
    p['ba'] = rg_b_a[l][None, :]
    p['bx'] = rg_b_x[l][None, :]
    p['lam'] = rg_lambda[l][None, :]
    p['gla_wa'] = jnp.pad(gla_w_a[l], ((0, SMALL_W - GLA_RANK), (0, 0))).astype(BF16)
    p['gla_ba'] = gla_b_a[l][None, :]
    p['gla_gn'] = gla_norm_g[l][None, :]
    p['qg'] = jnp.tile(nsa_q_norm_g[l], NSA_HEADS)[None, :]
    p['kg'] = jnp.tile(nsa_k_norm_g[l], (1, NSA_G))
    p['phk'] = _block_diag4(nsa_phi_k[l]).astype(BF16)
    p['phv'] = _block_diag4(nsa_phi_v[l]).astype(BF16)
    p['pek'] = jnp.tile(nsa_cmp_pe[l, 0], (1, NSA_G))
    p['pev'] = jnp.tile(nsa_cmp_pe[l, 1], (1, NSA_G))
    p['w_rg_out'] = w_rg_out[l].astype(BF16)
    p['w_gla_out'] = w_gla_out[l].astype(BF16)
    p['w_nsa_out'] = w_nsa_out[l].astype(BF16)
    p['w_out'] = w_out[l].astype(BF16)
    p['norm_ffn'] = norm_ffn_g[l][None, :]
    p['wg'] = w_ffn_gate[l].astype(BF16)
    p['wu'] = w_ffn_up[l].astype(BF16)
    p['wd'] = w_ffn_down[l].astype(BF16)
    del half
    return p


def _seg_ones():
    r = jnp.arange(128) // NSA_DH
    return (r[:, None] == r[None, :]).astype(BF16)


def _cmp_to_sel_t(n_cmp_rows, n_cmp_valid, n_sel):
    start = jnp.arange(n_cmp_rows)[None, :] * CMP_STRIDE
    sel = jnp.arange(n_sel)[:, None] * SEL_BLOCK
    ok = (start < sel + SEL_BLOCK) & (start + CMP_BLOCK > sel) & (jnp.arange(n_cmp_rows)[None, :] < n_cmp_valid)
    return ok.astype(BF16)


def _expand_blocks(n_sel, n_keys):
    key_blk = jnp.arange(n_keys) // SEL_BLOCK
    e = (jnp.arange(n_sel)[:, None] == key_blk[None, :]).astype(BF16)
    return e.reshape(n_sel, n_keys // KEY_TILE, KEY_TILE).transpose(1, 0, 2)


def _slope_cols(rows_per_head):
    return jnp.repeat(_alibi_slopes(), rows_per_head, axis=1)[:, :, None]


def _prompt_layer(x, p):
    b, t, _ = x.shape
    n = b * t
    z2 = _inproj(x.reshape(n, D_MODEL), p['norm_mix'], p['w_in'])
    z3 = z2.reshape(b, t, ZC)
    o_a, h_last = _rglru_prompt(z3, p['cw'], p['cb'], p['wax'], p['ba'], p['bx'], p['lam'])
    o_b, st = _gla_prompt(z3, p['gla_wa'], p['gla_ba'], p['gla_gn'])
    seg = _seg_ones()
    q_hm, slc_n, ks, vs, win_n, kw, vw, gates = _nsa_prep(z3, p['qg'], p['kg'], seg)
    nch = t // CMP_STRIDE
    cmp_flat = z3[:, :, C_NCMP:C_NCMP + KV_W]
    kc, vc = _compress_prompt(cmp_flat.reshape(b, nch, CMP_STRIDE * KV_W), p['phk'], p['phv'],
                              p['pek'], p['pev'], p['kg'], seg)
    n_sel = t // SEL_BLOCK
    m2s_t = _cmp_to_sel_t(nch, nch - 1, n_sel)
    slope_col = _slope_cols(SEL_BLOCK)
    o_cmp, sel = _cmp_attn_prompt(q_hm, kc, vc, gates, slope_col, m2s_t)
    expand = _expand_blocks(n_sel, t)
    o_slc = _tile_attn_prompt(q_hm, ks, vs, gates, slope_col, sel, expand, "slc")
    o_win = _tile_attn_prompt(q_hm, kw, vw, gates, slope_col, sel, expand, "win")
    x2 = _merge(x.reshape(n, D_MODEL), z2, o_a.reshape(n, D_MODEL), o_b.reshape(n, D_MODEL),
                o_cmp.reshape(n, D_MODEL), o_slc.reshape(n, D_MODEL), o_win.reshape(n, D_MODEL),
                p['w_rg_out'], p['w_gla_out'], p['w_nsa_out'], p['w_out'])
    y = _ffn(x2, p['norm_ffn'], p['wg'], p['wu'], p['wd']).reshape(b, t, D_MODEL)
    kv_shape = (b, t, 2, NSA_G, NSA_DH)
    cmp_kv = cmp_flat.reshape(kv_shape)
    slc_kv = slc_n.reshape(kv_shape)
    win_state = win_n[:, t - min(WINDOW, t):].reshape(b, min(WINDOW, t), 2, NSA_G, NSA_DH)
    s_new = jnp.swapaxes(st, -1, -2)
    conv_new = z3[:, t - (RG_CONV_W - 1):, C_RGX:C_RGX + D_MODEL]
    return y, (cmp_kv, slc_kv, win_state, s_new, h_last[:, 7], conv_new)


def _rglru_step_kernel(y_ref, x_ref, c0_ref, c1_ref, c2_ref, h0_ref, cw_ref, cb_ref, wax_ref,
                       ba_ref, bx_ref, lam_ref, o_ref, h_ref):
    cw = cw_ref[...]
    x = x_ref[...]
    xc = (cb_ref[...] + c0_ref[...] * cw[0:1] + c1_ref[...] * cw[1:2]
          + c2_ref[...] * cw[2:3] + x * cw[3:4])
    a_parts, u_parts = _rglru_gates(xc, wax_ref, ba_ref[...], bx_ref[...], lam_ref[...])
    a = jnp.concatenate(a_parts, axis=1)
    u = jnp.concatenate(u_parts, axis=1)
    h = u + a * h0_ref[...]
    h_ref[...] = h
    o_ref[...] = _gelu_tanh(y_ref[...]) * h


def _rglru_sample(z2, conv_state, h0, cw, cb, wax, ba, bx, lam):
    n = z2.shape[0]
    tok = lambda c: pl.BlockSpec((n, D_MODEL), functools.partial(lambda i, c: (0, c), c=c))
    full = pl.BlockSpec((n, D_MODEL), lambda i: (0, 0))
    row = pl.BlockSpec((1, D_MODEL), lambda i: (0, 0))
    return pl.pallas_call(
        _rglru_step_kernel,
        grid=(1,),
        in_specs=[tok(C_RGY // D_MODEL), tok(C_RGX // D_MODEL), full, full, full, full,
                  pl.BlockSpec((RG_CONV_W, D_MODEL), lambda i: (0, 0)), row,
                  pl.BlockSpec((RG_BLOCKS, RG_BLOCK_W, 2 * RG_BLOCK_W), lambda i: (0, 0, 0)),
                  row, row, row],
        out_specs=[full, full],
        out_shape=[jax.ShapeDtypeStruct((n, D_MODEL), F32)] * 2,
        compiler_params=_cp(("arbitrary",)),
        name="rglru_sample",
    )(z2, z2, conv_state[:, 0], conv_state[:, 1], conv_state[:, 2], h0, cw, cb, wax, ba, bx, lam)


def _row_to_col(row):
    n = row.shape[1]
    eye = lax.broadcasted_iota(jnp.int32, (n, n), 0) == lax.broadcasted_iota(jnp.int32, (n, n), 1)
    return jnp.sum(jnp.where(eye, jnp.broadcast_to(row, (n, n)), 0.0), axis=1, keepdims=True)


def _gla_step_kernel(q_ref, k_ref, v_ref, r_ref, ga_ref, s_ref, wa_ref, ba_ref, gn_ref,
                     o_ref, so_ref):
    ga = jnp.broadcast_to(ga_ref[0], (8, SMALL_W)).astype(BF16)
    xg = _dot(ga, wa_ref[...])[0:1] + ba_ref[...]
    lg = _log_sigmoid(xg) * (1.0 / GLA_TAU)
    for h in range(GLA_HEADS):
        ksl = slice(h * GLA_DK, (h + 1) * GLA_DK)
        vsl = slice(h * GLA_DV, (h + 1) * GLA_DV)
        q = q_ref[0, :, ksl] * (GLA_DK ** -0.5)
        k = k_ref[0, :, ksl]
        v = v_ref[0, :, vsl]
        b = lg[:, ksl]
        eb = jnp.exp(b)
        s_old = s_ref[0, h]
        a = jnp.sum(q * k, axis=-1, keepdims=True)
        qe = jnp.broadcast_to(q * eb, (8, GLA_DK)).astype(BF16)
        o = a * v + _dot(qe, s_old.astype(BF16))[0:1]
        so_ref[0, h] = _row_to_col(eb) * s_old + _row_to_col(k) * v
        ms = jnp.mean(o * o, axis=-1, keepdims=True)
        on = o * lax.rsqrt(ms + EPS) * gn_ref[...]
        o_ref[0, :, vsl] = on * _silu(r_ref[0, :, vsl])


def _gla_sample(z3, state, wa_pad, ba, gn):
    n = z3.shape[0]
    hk = GLA_HEADS * GLA_DK
    hv = GLA_HEADS * GLA_DV
    row = lambda i: (0, 0)
    st = pl.BlockSpec((1, GLA_HEADS, GLA_DK, GLA_DV), lambda i: (i, 0, 0, 0))
    return pl.pallas_call(
        _gla_step_kernel,
        grid=(n,),
        in_specs=[
            pl.BlockSpec((1, 1, hk), lambda i: (i, 0, C_GQ // hk)),
            pl.BlockSpec((1, 1, hk), lambda i: (i, 0, C_GK // hk)),
            pl.BlockSpec((1, 1, hv), lambda i: (i, 0, C_GV // hv)),
            pl.BlockSpec((1, 1, hv), lambda i: (i, 0, C_GR // hv)),
            pl.BlockSpec((1, 1, SMALL_W), lambda i: (i, 0, C_SMALL // SMALL_W)),
            st,
            pl.BlockSpec((SMALL_W, hk), row),
            pl.BlockSpec((1, hk), row),
            pl.BlockSpec((1, GLA_DV), row),
        ],
        out_specs=[pl.BlockSpec((1, 1, hv), lambda i: (i, 0, 0)), st],
        out_shape=[jax.ShapeDtypeStruct((n, 1, hv), F32),
                   jax.ShapeDtypeStruct(state.shape, F32)],
        compiler_params=_cp(("arbitrary",)),
        name="gla_sample",
    )(z3, z3, z3, z3, z3, state, wa_pad, ba, gn)


def _group_fold(o):
    grp = lax.broadcasted_iota(jnp.int32, (NSA_HEADS, NSA_DH), 0) // NSA_P
    out = jnp.zeros((NSA_HEADS, NSA_DH), F32)
    for g in range(NSA_G):
        out = out + jnp.where(grp == g, o[:, g * NSA_DH:(g + 1) * NSA_DH], 0.0)
    return out


def _nsa_sample_kernel(pt_ref, q_ref, slc_new_ref, win_new_ref, sm_ref, wb_ref, sl_ref,
                       phk_ref, phv_ref, pek_ref, pev_ref, kg_ref, seg_ref, m2s_ref, ex_ref,
                       cmp_pool, slc_pool, o_ref, wo_ref, buf, sh_scr, sem,
                       *, layer_off, n_keep):
    bi = pl.program_id(0)
    n_pages = pt_ref.shape[1]
    past = n_pages * PAGE_SIZE
    half = NSA_G * NSA_DH
    nch = past // CMP_STRIDE

    cpp = PAGE_SIZE // CMP_STRIDE

    def page_copy(pool, pg):
        return pltpu.make_async_copy(pool.at[layer_off + pt_ref[bi, pg]],
                                     buf.at[pl.ds(pg * cpp, cpp)], sem)

    def gather(pool):
        def start(pg, c):
            page_copy(pool, pg).start()
            return c

        def wait(pg, c):
            page_copy(pool, pg).wait()
            return c

        lax.fori_loop(0, n_pages, start, 0)
        lax.fori_loop(0, n_pages, wait, 0)

    q16 = q_ref[0]
    head = lax.broadcasted_iota(jnp.int32, (NSA_HEADS, half), 0)
    lane = lax.broadcasted_iota(jnp.int32, (NSA_HEADS, half), 1)
    qblk = jnp.where(lane // NSA_DH == head // NSA_P,
                     jnp.concatenate([q16] * NSA_G, axis=1), jnp.zeros((), BF16))
    slope = sl_ref[...]
    sig = jax.nn.sigmoid(sm_ref[0])
    lane_s = lax.broadcasted_iota(jnp.int32, (NSA_HEADS, SMALL_W), 1)
    head_s = lax.broadcasted_iota(jnp.int32, (NSA_HEADS, SMALL_W), 0)

    def gate_col(c):
        return jnp.sum(jnp.where(lane_s == GATE_OFF + 3 * head_s + c, sig, 0.0),
                       axis=1, keepdims=True)

    gather(cmp_pool)
    kc, vc = _compress_rows(lambda s: buf[:, s * KV_W:(s + 1) * KV_W], nch,
                            phk_ref, phv_ref, pek_ref, pev_ref, sh_scr)
    kn = kc * lax.rsqrt(_seg_mean_sq(kc, seg_ref[...]) + EPS) * kg_ref[0:1, :]
    s = _nt(qblk, kn.astype(BF16))
    c_end = lax.broadcasted_iota(jnp.int32, (NSA_HEADS, nch), 1) * CMP_STRIDE + (CMP_BLOCK - 1)
    mask = c_end <= past
    sc = jnp.where(mask, s - slope * (past - c_end).astype(F32), NEG)
    m = jnp.max(sc, axis=-1, keepdims=True)
    e = jnp.where(mask, jnp.exp(sc - m), 0.0)
    p = e / jnp.maximum(jnp.sum(e, axis=-1, keepdims=True), TINY)
    o_cmp = _group_fold(_dot(p.astype(BF16), vc.astype(BF16)))
    ps_rows = []
    for g in range(NSA_G):
        r = p[g * NSA_P:g * NSA_P + 1]
        for pp in range(1, NSA_P):
            r = r + p[g * NSA_P + pp:g * NSA_P + pp + 1]
        ps_rows.append(r)
    ps = jnp.concatenate(ps_rows + [jnp.zeros((8 - NSA_G, nch), F32)], axis=0)
    hi = ps.astype(BF16)
    lo = (ps - hi.astype(F32)).astype(BF16)
    imp = _dot(hi, m2s_ref[...]) + _dot(lo, m2s_ref[...])
    nselp = imp.shape[1]
    cur = past // SEL_BLOCK
    jl = lax.broadcasted_iota(jnp.int32, (8, nselp), 1)
    forced = (jl == 0) | (jl == cur) | (jl == cur - 1)
    score = jnp.where(jl <= cur, jnp.where(forced, FORCE_SCORE, imp), NEG)
    jr = lax.broadcasted_iota(jnp.int32, (nselp, nselp), 0)
    jc = lax.broadcasted_iota(jnp.int32, (nselp, nselp), 1)
    sel_rows = []
    for g in range(NSA_G):
        row = score[g:g + 1]
        col = _row_to_col(row)
        ahead = jnp.where(jc > jr, jnp.where(col >= row, 1.0, 0.0), jnp.where(col > row, 1.0, 0.0))
        rank = jnp.sum(ahead, axis=0, keepdims=True)
        sel_g = jnp.where(rank < n_keep, 1.0, 0.0)
        sel_rows += [sel_g] * NSA_P
    sel16 = jnp.concatenate(sel_rows, axis=0).astype(BF16)

    gather(slc_pool)
    m = jnp.full((NSA_HEADS, 1), NEG, F32)
    l = jnp.zeros((NSA_HEADS, 1), F32)
    acc = jnp.zeros((NSA_HEADS, half), F32)
    mask = _dot(sel16, ex_ref[...]) > 0.5
    chunk0 = lax.broadcasted_iota(jnp.int32, (NSA_HEADS, nch), 1) * CMP_STRIDE
    for t in range(CMP_STRIDE):
        kk = buf[:, t * KV_W:t * KV_W + half].astype(BF16)
        vv = buf[:, t * KV_W + half:(t + 1) * KV_W].astype(BF16)
        s = _nt(qblk, kk)
        sc = jnp.where(mask, s - slope * (past - (chunk0 + t)).astype(F32), NEG)
        m_new = jnp.maximum(m, jnp.max(sc, axis=-1, keepdims=True))
        alpha = jnp.exp(m - m_new)
        e = jnp.where(mask, jnp.exp(sc - m_new), 0.0)
        l = alpha * l + jnp.sum(e, axis=-1, keepdims=True)
        acc = alpha * acc + _dot(e.astype(BF16), vv)
        m = m_new
    qf = qblk.astype(F32)

    def add_self(m, l, acc, row):
        k_new = row[:, 0:half].astype(BF16).astype(F32)
        v_new = row[:, half:2 * half].astype(BF16).astype(F32)
        s_new = jnp.sum(qf * k_new, axis=-1, keepdims=True)
        m_new = jnp.maximum(m, s_new)
        alpha = jnp.exp(m - m_new)
        e_new = jnp.exp(s_new - m_new)
        l = alpha * l + e_new
        acc = alpha * acc + e_new.astype(BF16).astype(F32) * v_new
        return l, acc

    l, acc = add_self(m, l, acc, slc_new_ref[0])
    o_slc = _group_fold(acc / jnp.maximum(l, TINY))

    w_len = wb_ref.shape[1]
    kk = wb_ref[0, :, 0:half].astype(BF16)
    vv = wb_ref[0, :, half:2 * half].astype(BF16)
    s = _nt(qblk, kk)
    dist = w_len - lax.broadcasted_iota(jnp.int32, (NSA_HEADS, w_len), 1)
    mask = dist <= WINDOW
    sc = jnp.where(mask, s - slope * dist.astype(F32), NEG)
    m = jnp.max(sc, axis=-1, keepdims=True)
    e = jnp.where(mask, jnp.exp(sc - m), 0.0)
    l = jnp.sum(e, axis=-1, keepdims=True)
    acc = _dot(e.astype(BF16), vv)
    l, acc = add_self(m, l, acc, win_new_ref[0])
    o_win = _group_fold(acc / jnp.maximum(l, TINY))

    o_ref[0] = gate_col(0) * o_cmp + gate_col(1) * o_slc + gate_col(2) * o_win
    wo_ref[0, 0:w_len - 1, :] = wb_ref[0, 1:w_len, :]
    wo_ref[0, w_len - 1:w_len, :] = win_new_ref[0]


def _nsa_sample(page_table, q_bhd, slc_new, win_new, z3, win_buf, cmp_pool, slc_pool, layer_off,
                p, seg):
    n, n_pages = page_table.shape
    past = n_pages * PAGE_SIZE
    half = NSA_G * NSA_DH
    nch = past // CMP_STRIDE
    n_sel = past // SEL_BLOCK + 1
    nselp = -(-n_sel // 128) * 128
    w_len = win_buf.shape[1]
    m2s = jnp.pad(_cmp_to_sel_t(nch, nch - 1, n_sel).T, ((0, 0), (0, nselp - n_sel)))
    chunk_blk = (jnp.arange(nch) * CMP_STRIDE) // SEL_BLOCK
    expand = (jnp.arange(nselp)[:, None] == chunk_blk[None, :]).astype(BF16)
    slope16 = _alibi_slopes().reshape(NSA_HEADS, 1)
    c2 = lambda i, pt: (0, 0)
    c3 = lambda i, pt: (0, 0, 0)
    grid_spec = pltpu.PrefetchScalarGridSpec(
        num_scalar_prefetch=1,
        grid=(n,),
        in_specs=[
            pl.BlockSpec((1, NSA_HEADS, NSA_DH), lambda i, pt: (i, 0, 0)),
            pl.BlockSpec((1, 1, KV_W), lambda i, pt: (i, 0, 0)),
            pl.BlockSpec((1, 1, KV_W), lambda i, pt: (i, 0, 0)),
            pl.BlockSpec((1, 1, SMALL_W), lambda i, pt: (i, 0, C_SMALL // SMALL_W)),
            pl.BlockSpec((1, w_len, KV_W), lambda i, pt: (i, 0, 0)),
            pl.BlockSpec((NSA_HEADS, 1), c2),
            pl.BlockSpec((CMP_BLOCK, half, half), c3),
            pl.BlockSpec((CMP_BLOCK, half, half), c3),
            pl.BlockSpec((CMP_BLOCK, half), c2),
            pl.BlockSpec((CMP_BLOCK, half), c2),
            pl.BlockSpec((3, half), c2),
            pl.BlockSpec((128, 128), c2),
            pl.BlockSpec((nch, nselp), c2),
            pl.BlockSpec((nselp, nch), c2),
            pl.BlockSpec(memory_space=pl.ANY),
            pl.BlockSpec(memory_space=pl.ANY),
        ],
        out_specs=[
            pl.BlockSpec((1, NSA_HEADS, NSA_DH), lambda i, pt: (i, 0, 0)),
            pl.BlockSpec((1, w_len, KV_W), lambda i, pt: (i, 0, 0)),
        ],
        scratch_shapes=[
            pltpu.VMEM((nch, CMP_STRIDE * KV_W), F32),
            pltpu.VMEM((nch + 8, half), F32),
            pltpu.SemaphoreType.DMA(()),
        ],
    )
    return pl.pallas_call(
        functools.partial(_nsa_sample_kernel, layer_off=layer_off, n_keep=min(TOP_N, n_sel)),
        grid_spec=grid_spec,
        out_shape=[jax.ShapeDtypeStruct((n, NSA_HEADS, NSA_DH), F32),
                   jax.ShapeDtypeStruct((n, w_len, KV_W), F32)],
        compiler_params=_cp(("arbitrary",)),
        name="nsa_sample",
    )(page_table, q_bhd, slc_new, win_new, z3, win_buf, slope16, p['phk'], p['phv'], p['pek'],
      p['pev'], p['kg'], seg, m2s, expand, cmp_pool, slc_pool)


def _sample_layer(x, p, l, cmp_pool, slc_pool, win_buf, state_gla, state_rglru, state_conv,
                  page_table, n_pool):
    n = x.shape[0]
    xs = x.reshape(n, D_MODEL)
    z2 = _inproj(xs, p['norm_mix'], p['w_in'])
    z3 = z2.reshape(n, 1, ZC)
    o_a, h_new = _rglru_sample(z2, state_conv, state_rglru, p['cw'], p['cb'], p['wax'], p['ba'],
                               p['bx'], p['lam'])
    o_b, s_new = _gla_sample(z3, state_gla, p['gla_wa'], p['gla_ba'], p['gla_gn'])
    seg = _seg_ones()
    q_hm, slc_n, _, _, win_n, _, _, _ = _nsa_prep(z2.reshape(1, n, ZC), p['qg'], p['kg'], seg)
    q_bhd = jnp.swapaxes(q_hm[0], 0, 1)
    slc_new = slc_n.reshape(n, 1, KV_W)
    win_new = win_n.reshape(n, 1, KV_W)
    w_len = win_buf.shape[1]
    o_c, win_state = _nsa_sample(page_table, q_bhd, slc_new, win_new, z3,
                                 win_buf.reshape(n, w_len, KV_W), cmp_pool, slc_pool, l * n_pool,
                                 p, seg)
    zero = jnp.zeros((n, D_MODEL), F32)
    x2 = _merge(xs, z2, o_a, o_b.reshape(n, D_MODEL), o_c.reshape(n, D_MODEL), zero, zero,
                p['w_rg_out'], p['w_gla_out'], p['w_nsa_out'], p['w_out'])
    y = _ffn(x2, p['norm_ffn'], p['wg'], p['wu'], p['wd']).reshape(n, 1, D_MODEL)
    kv_shape = (n, 1, 2, NSA_G, NSA_DH)
    cmp_kv = z2[:, C_NCMP:C_NCMP + KV_W].reshape(kv_shape)
    conv_new = jnp.concatenate([state_conv[:, 1:], z2[:, None, C_RGX:C_RGX + D_MODEL]], axis=1)
    return y, (cmp_kv, slc_n.reshape(kv_shape), win_state.reshape(n, w_len, 2, NSA_G, NSA_DH),
               s_new, h_new, conv_new)


def kernel(x_prompt, x_sample, cache_cmp_kv, cache_slc_kv, cache_win_kv, state_gla, state_rglru,
           state_conv, page_table, norm_mix_g, w_in, rg_conv_w, rg_conv_b, rg_w_a, rg_b_a, rg_w_x,
           rg_b_x, rg_lambda, gla_w_a, gla_b_a, gla_norm_g, nsa_q_norm_g, nsa_k_norm_g, nsa_phi_k,
           nsa_phi_v, nsa_cmp_pe, w_rg_out, w_gla_out, w_nsa_out, w_out, norm_ffn_g, w_ffn_gate,
           w_ffn_up, w_ffn_down):
    depth, n_pool = cache_cmp_kv.shape[:2]
    pool_shape = (depth * n_pool, PAGE_SIZE // CMP_STRIDE, CMP_STRIDE * KV_W)
    cmp_pool = cache_cmp_kv.reshape(pool_shape)
    slc_pool = cache_slc_kv.reshape(pool_shape)
    yp, ys = x_prompt, x_sample
    p_out = [[] for _ in range(6)]
    s_out = [[] for _ in range(6)]
    for l in range(depth):
        p = _layer_params(l, norm_mix_g, w_in, rg_conv_w, rg_conv_b, rg_w_a, rg_b_a, rg_w_x,
                          rg_b_x, rg_lambda, gla_w_a, gla_b_a, gla_norm_g, nsa_q_norm_g,
                          nsa_k_norm_g, nsa_phi_k, nsa_phi_v, nsa_cmp_pe, w_rg_out, w_gla_out,
                          w_nsa_out, w_out, norm_ffn_g, w_ffn_gate, w_ffn_up, w_ffn_down)
        yp, p_new = _prompt_layer(yp, p)
        ys, s_new = _sample_layer(ys, p, l, cmp_pool, slc_pool, cache_win_kv[l], state_gla[l],
                                  state_rglru[l], state_conv[l], page_table, n_pool)
        for i in range(6):
            p_out[i].append(p_new[i])
            s_out[i].append(s_new[i])
    return ((yp, ys) + tuple(jnp.stack(a) for a in p_out) + tuple(jnp.stack(a) for a in s_out))
```

```python
import functools

import jax
import jax.numpy as jnp
from jax import lax
from jax.experimental import pallas as pl
from jax.experimental.pallas import tpu as pltpu

F32 = jnp.float32
BF16 = jnp.bfloat16

D_MODEL = 1024
PAGE_SIZE = 128
RG_BLOCKS = 8
RG_BLOCK_W = D_MODEL // RG_BLOCKS
RG_CONV_W = 4
RG_C = 8.0
GLA_HEADS = 4
GLA_DK = 128
GLA_DV = 256
GLA_RANK = 16
GLA_TAU = 16.0
GLA_CHUNK = 64
GLA_SUB = 16
NSA_HEADS = 16
NSA_G = 4
NSA_P = 4
NSA_DH = 64
CMP_BLOCK = 32
CMP_STRIDE = 16
SEL_BLOCK = 64
TOP_N = 16
WINDOW = 512
D_FF = 2816
KV_W = 512
EPS = 1e-6
NEG = -1e30
TINY = 1e-30
FORCE_SCORE = 1e4

COL_TILE = 512
C_MGATE = 0
C_RGY = 3072
C_RGX = 4096
C_GQ = 5120
C_GK = 5632
C_GV = 6144
C_GR = 7168
C_NQ = 8192
C_NCMP = 9216
C_NSLC = 9728
C_NWIN = 10240
C_SMALL = 10752
ZC = 11264
SMALL_W = 128
GATE_OFF = GLA_RANK

VMEM_LIMIT = 56 * 1024 * 1024


def _cp(sem, vmem=VMEM_LIMIT):
    return pltpu.CompilerParams(dimension_semantics=sem, vmem_limit_bytes=vmem)


def _nt(a, b):
    return lax.dot_general(a, b, (((1,), (1,)), ((), ())), preferred_element_type=F32)


def _tn(a, b):
    return lax.dot_general(a, b, (((0,), (0,)), ((), ())), preferred_element_type=F32)


def _dot(a, b):
    return jnp.dot(a, b, preferred_element_type=F32)


def _softplus(x):
    return jnp.maximum(x, 0.0) + jnp.log1p(jnp.exp(-jnp.abs(x)))


def _log_sigmoid(x):
    return jnp.minimum(x, 0.0) - jnp.log1p(jnp.exp(-jnp.abs(x)))


def _gelu_tanh(x):
    c = 0.7978845608028654
    return x * (0.5 * (1.0 + jnp.tanh(c * (x + 0.044715 * (x * x * x)))))


def _silu(x):
    return x * jax.nn.sigmoid(x)


def _seg_mean_sq(x, seg_ones):
    x2 = x * x
    hi = x2.astype(BF16)
    lo = (x2 - hi.astype(F32)).astype(BF16)
    outs = []
    for c in range(x.shape[1] // 128):
        sl = slice(c * 128, (c + 1) * 128)
        outs.append(_dot(hi[:, sl], seg_ones) + _dot(lo[:, sl], seg_ones))
    y = outs[0] if len(outs) == 1 else jnp.concatenate(outs, axis=1)
    return y * (1.0 / NSA_DH)


def _inproj_kernel(x_ref, g_ref, w_ref, o_ref, h_scr):
    @pl.when(pl.program_id(1) == 0)
    def _():
        x = x_ref[...]
        ms = jnp.mean(x * x, axis=-1, keepdims=True)
        h_scr[...] = (x * lax.rsqrt(ms + EPS) * g_ref[...]).astype(BF16)

    o_ref[...] = _dot(h_scr[...], w_ref[...])


def _inproj(x2d, g, w_perm):
    n = x2d.shape[0]
    tm = min(n, 1024)
    return pl.pallas_call(
        _inproj_kernel,
        grid=(n // tm, ZC // COL_TILE),
        in_specs=[
            pl.BlockSpec((tm, D_MODEL), lambda i, j: (i, 0)),
            pl.BlockSpec((1, D_MODEL), lambda i, j: (0, 0)),
            pl.BlockSpec((D_MODEL, COL_TILE), lambda i, j: (0, j)),
        ],
        out_specs=pl.BlockSpec((tm, COL_TILE), lambda i, j: (i, j)),
        out_shape=jax.ShapeDtypeStruct((n, ZC), F32),
        scratch_shapes=[pltpu.VMEM((tm, D_MODEL), BF16)],
        compiler_params=_cp(("arbitrary", "arbitrary")),
        name="inproj",
    )(x2d, g, w_perm)


def _rglru_gates(xc, wax_ref, ba, bx, lam):
    sp = _softplus(-lam)
    a_parts, u_parts = [], []
    for n in range(RG_BLOCKS):
        sl = slice(n * RG_BLOCK_W, (n + 1) * RG_BLOCK_W)
        xb = xc[:, sl]
        r = _dot(xb.astype(BF16), wax_ref[n])
        rec = jax.nn.sigmoid(r[:, :RG_BLOCK_W] + ba[:, sl])
        inp = jax.nn.sigmoid(r[:, RG_BLOCK_W:] + bx[:, sl])
        log_a = (-RG_C) * rec * sp[:, sl]
        a = jnp.exp(log_a)
        u = jnp.sqrt(-(jnp.tanh(log_a) * (a * a + 1.0))) * (inp * xb)
        a_parts.append(a)
        u_parts.append(u)
    return a_parts, u_parts


def _rglru_kernel(y_ref, x_ref, cw_ref, cb_ref, wax_ref, ba_ref, bx_ref, lam_ref,
                  o_ref, hl_ref, xp_scr, a_scr, u_scr, h_scr):
    t = pl.program_id(1)
    tt = x_ref.shape[1]

    @pl.when(t == 0)
    def _():
        xp_scr[0:8, :] = jnp.zeros((8, D_MODEL), F32)
        h_scr[...] = jnp.zeros_like(h_scr)

    x = x_ref[0]
    xp_scr[8:8 + tt, :] = x
    cw = cw_ref[...]
    xc = (cb_ref[...] + xp_scr[5:5 + tt, :] * cw[0:1] + xp_scr[6:6 + tt, :] * cw[1:2]
          + xp_scr[7:7 + tt, :] * cw[2:3] + x * cw[3:4])
    xp_scr[0:8, :] = xp_scr[tt:tt + 8, :]

    a_parts, u_parts = _rglru_gates(xc, wax_ref, ba_ref[...], bx_ref[...], lam_ref[...])
    for n in range(RG_BLOCKS):
        sl = slice(n * RG_BLOCK_W, (n + 1) * RG_BLOCK_W)
        a_scr[:, sl] = a_parts[n]
        u_scr[:, sl] = u_parts[n]

    row = lax.broadcasted_iota(jnp.int32, (8, D_MODEL), 0)

    def body(i, h):
        r0 = pl.multiple_of(i * 8, 8)
        a = a_scr[pl.ds(r0, 8), :]
        u = u_scr[pl.ds(r0, 8), :]
        for d in (1, 2, 4):
            a_sh = pltpu.roll(a, d, 0)
            u_sh = pltpu.roll(u, d, 0)
            m = row >= d
            u = jnp.where(m, a * u_sh + u, u)
            a = jnp.where(m, a * a_sh, a)
        hh = u + a * h
        u_scr[pl.ds(r0, 8), :] = hh
        return hh[7:8, :]

    h = lax.fori_loop(0, tt // 8, body, h_scr[0:1, :])
    h_scr[...] = jnp.broadcast_to(h, h_scr.shape)
    hl_ref[0] = jnp.broadcast_to(h, (8, D_MODEL))
    o_ref[0] = _gelu_tanh(y_ref[0]) * u_scr[...]


def _rglru_prompt(z3, cw, cb, wax, ba, bx, lam):
    b, t, _ = z3.shape
    tt = min(t, 512)
    row = lambda i, j: (0, 0)
    return pl.pallas_call(
        _rglru_kernel,
        grid=(b, t // tt),
        in_specs=[
            pl.BlockSpec((1, tt, D_MODEL), lambda i, j: (i, j, C_RGY // D_MODEL)),
            pl.BlockSpec((1, tt, D_MODEL), lambda i, j: (i, j, C_RGX // D_MODEL)),
            pl.BlockSpec((RG_CONV_W, D_MODEL), row),
            pl.BlockSpec((1, D_MODEL), row),
            pl.BlockSpec((RG_BLOCKS, RG_BLOCK_W, 2 * RG_BLOCK_W), lambda i, j: (0, 0, 0)),
            pl.BlockSpec((1, D_MODEL), row),
            pl.BlockSpec((1, D_MODEL), row),
            pl.BlockSpec((1, D_MODEL), row),
        ],
        out_specs=[
            pl.BlockSpec((1, tt, D_MODEL), lambda i, j: (i, j, 0)),
            pl.BlockSpec((1, 8, D_MODEL), lambda i, j: (i, 0, 0)),
        ],
        out_shape=[
            jax.ShapeDtypeStruct((b, t, D_MODEL), F32),
            jax.ShapeDtypeStruct((b, 8, D_MODEL), F32),
        ],
        scratch_shapes=[
            pltpu.VMEM((tt + 8, D_MODEL), F32),
            pltpu.VMEM((tt, D_MODEL), F32),
            pltpu.VMEM((tt, D_MODEL), F32),
            pltpu.VMEM((8, D_MODEL), F32),
        ],
        compiler_params=_cp(("arbitrary", "arbitrary")),
        name="rglru_prompt",
    )(z3, z3, cw, cb, wax, ba, bx, lam)


def _gla_chunk_head(q, k, v, lg, st):
    c = GLA_CHUNK
    r_i = lax.broadcasted_iota(jnp.int32, (c, c), 0)
    c_i = lax.broadcasted_iota(jnp.int32, (c, c), 1)
    tril = (r_i >= c_i).astype(F32)
    b = jnp.dot(tril, lg, preferred_element_type=F32, precision=lax.Precision.HIGHEST)
    kb = k.astype(BF16)
    vb = v.astype(BF16)
    o_state = _nt((q * jnp.exp(b)).astype(BF16), st.astype(BF16))
    sub = GLA_SUB
    t_row = lax.broadcasted_iota(jnp.int32, (sub, GLA_DK), 0)
    lane = lax.broadcasted_iota(jnp.int32, (sub, GLA_DK), 1)
    o_parts = []
    for blk in range(c // sub):
        r0 = blk * sub
        q_i = q[r0:r0 + sub]
        k_i = k[r0:r0 + sub]
        b_i = b[r0:r0 + sub]
        a_d = jnp.zeros((sub, GLA_DK), F32)
        for s in range(sub):
            diff = b_i - b_i[s:s + 1]
            dec = jnp.exp(jnp.where(t_row >= s, diff, NEG))
            col = jnp.sum(q_i * k_i[s:s + 1] * dec, axis=-1, keepdims=True)
            a_d = jnp.where(lane == s, col, a_d)
        o_i = _dot(a_d[:, :sub].astype(BF16), vb[r0:r0 + sub])
        if blk > 0:
            b_ref = b[r0 - 1:r0]
            qd = (q_i * jnp.exp(b_i - b_ref)).astype(BF16)
            kd = (k[:r0] * jnp.exp(b_ref - b[:r0])).astype(BF16)
            a_off = _nt(qd, kd)
            o_i = o_i + _dot(a_off.astype(BF16), vb[:r0])
        o_parts.append(o_i)
    o = o_state + jnp.concatenate(o_parts, axis=0)
    b_last = b[c - 1:c]
    kd = (k * jnp.exp(b_last - b)).astype(BF16)
    st_new = st * jnp.exp(b_last) + _tn(vb, kd)
    del kb
    return o, st_new


def _gla_kernel(q_ref, k_ref, v_ref, r_ref, ga_ref, wa_ref, ba_ref, gn_ref,
                o_ref, st_ref, st_scr):
    t = pl.program_id(1)
    tt = q_ref.shape[1]

    @pl.when(t == 0)
    def _():
        st_scr[...] = jnp.zeros_like(st_scr)

    gn = gn_ref[...]

    def body(ci, carry):
        r0 = pl.multiple_of(ci * GLA_CHUNK, GLA_CHUNK)
        rows = pl.ds(r0, GLA_CHUNK)
        xg = _dot(ga_ref[0, rows, :].astype(BF16), wa_ref[...]) + ba_ref[...]
        lg_all = _log_sigmoid(xg) * (1.0 / GLA_TAU)
        for h in range(GLA_HEADS):
            ksl = slice(h * GLA_DK, (h + 1) * GLA_DK)
            vsl = slice(h * GLA_DV, (h + 1) * GLA_DV)
            q = q_ref[0, rows, ksl] * (GLA_DK ** -0.5)
            k = k_ref[0, rows, ksl]
            v = v_ref[0, rows, vsl]
            o, st_new = _gla_chunk_head(q, k, v, lg_all[:, ksl], st_scr[h])
            st_scr[h] = st_new
            ms = jnp.mean(o * o, axis=-1, keepdims=True)
            on = o * lax.rsqrt(ms + EPS) * gn
            o_ref[0, rows, vsl] = on * _silu(r_ref[0, rows, vsl])
        return carry

    lax.fori_loop(0, tt // GLA_CHUNK, body, 0)
    st_ref[0] = st_scr[...]


def _gla_prompt(z3, wa_pad, ba, gn):
    b, t, _ = z3.shape
    tt = min(t, 256)
    hk = GLA_HEADS * GLA_DK
    hv = GLA_HEADS * GLA_DV
    row = lambda i, j: (0, 0)
    return pl.pallas_call(
        _gla_kernel,
        grid=(b, t // tt),
        in_specs=[
            pl.BlockSpec((1, tt, hk), lambda i, j: (i, j, C_GQ // hk)),
            pl.BlockSpec((1, tt, hk), lambda i, j: (i, j, C_GK // hk)),
            pl.BlockSpec((1, tt, hv), lambda i, j: (i, j, C_GV // hv)),
            pl.BlockSpec((1, tt, hv), lambda i, j: (i, j, C_GR // hv)),
            pl.BlockSpec((1, tt, SMALL_W), lambda i, j: (i, j, C_SMALL // SMALL_W)),
            pl.BlockSpec((SMALL_W, hk), row),
            pl.BlockSpec((1, hk), row),
            pl.BlockSpec((1, GLA_DV), row),
        ],
        out_specs=[
            pl.BlockSpec((1, tt, hv), lambda i, j: (i, j, 0)),
            pl.BlockSpec((1, GLA_HEADS, GLA_DV, GLA_DK), lambda i, j: (i, 0, 0, 0)),
        ],
        out_shape=[
            jax.ShapeDtypeStruct((b, t, hv), F32),
            jax.ShapeDtypeStruct((b, GLA_HEADS, GLA_DV, GLA_DK), F32),
        ],
        scratch_shapes=[pltpu.VMEM((GLA_HEADS, GLA_DV, GLA_DK), F32)],
        compiler_params=_cp(("arbitrary", "arbitrary")),
        name="gla_prompt",
    )(z3, z3, z3, z3, z3, wa_pad, ba, gn)


def _nsa_prep_kernel(nq_ref, slc_ref, win_ref, sm_ref, qg_ref, kg_ref, seg_ref,
                     q_ref, slc_o, ks_ref, vs_ref, win_o, kw_ref, vw_ref, gt_ref):
    seg = seg_ref[...]
    half = NSA_G * NSA_DH
    x = nq_ref[0]
    qn = x * lax.rsqrt(_seg_mean_sq(x, seg) + EPS) * qg_ref[...] * (NSA_DH ** -0.5)
    for h in range(NSA_HEADS):
        q_ref[0, h] = qn[:, h * NSA_DH:(h + 1) * NSA_DH].astype(BF16)
    for idx, (src, dst, kh, vh) in enumerate(((slc_ref, slc_o, ks_ref, vs_ref),
                                              (win_ref, win_o, kw_ref, vw_ref))):
        kv = src[0]
        kk = kv[:, :half]
        vv = kv[:, half:]
        kn = kk * lax.rsqrt(_seg_mean_sq(kk, seg) + EPS) * kg_ref[idx + 1:idx + 2, :]
        dst[0, :, :half] = kn
        dst[0, :, half:] = vv
        for g in range(NSA_G):
            sl = slice(g * NSA_DH, (g + 1) * NSA_DH)
            kh[0, g] = kn[:, sl].astype(BF16)
            vh[0, g] = vv[:, sl].astype(BF16)
    sig = jax.nn.sigmoid(sm_ref[0])
    for g in range(NSA_G):
        lo = GATE_OFF + g * NSA_P * 3
        gt_ref[0, g] = sig[:, lo:lo + NSA_P * 3]


def _nsa_prep(z3, qg_t, kg_t, seg_ones):
    b, t, _ = z3.shape
    tt = min(t, 512)
    row = lambda i, j: (0, 0)
    hm_q = pl.BlockSpec((1, NSA_HEADS, tt, NSA_DH), lambda i, j: (i, 0, j, 0))
    hm_kv = pl.BlockSpec((1, NSA_G, tt, NSA_DH), lambda i, j: (i, 0, j, 0))
    flat = pl.BlockSpec((1, tt, KV_W), lambda i, j: (i, j, 0))
    sd = jax.ShapeDtypeStruct
    return pl.pallas_call(
        _nsa_prep_kernel,
        grid=(b, t // tt),
        in_specs=[
            pl.BlockSpec((1, tt, D_MODEL), lambda i, j: (i, j, C_NQ // D_MODEL)),
            pl.BlockSpec((1, tt, KV_W), lambda i, j: (i, j, C_NSLC // KV_W)),
            pl.BlockSpec((1, tt, KV_W), lambda i, j: (i, j, C_NWIN // KV_W)),
            pl.BlockSpec((1, tt, SMALL_W), lambda i, j: (i, j, C_SMALL // SMALL_W)),
            pl.BlockSpec((1, D_MODEL), row),
            pl.BlockSpec((3, NSA_G * NSA_DH), row),
            pl.BlockSpec((128, 128), row),
        ],
        out_specs=[hm_q, flat, hm_kv, hm_kv, flat, hm_kv, hm_kv,
                   pl.BlockSpec((1, NSA_G, tt, NSA_P * 3), lambda i, j: (i, 0, j, 0))],
        out_shape=[
            sd((b, NSA_HEADS, t, NSA_DH), BF16),
            sd((b, t, KV_W), F32),
            sd((b, NSA_G, t, NSA_DH), BF16),
            sd((b, NSA_G, t, NSA_DH), BF16),
            sd((b, t, KV_W), F32),
            sd((b, NSA_G, t, NSA_DH), BF16),
            sd((b, NSA_G, t, NSA_DH), BF16),
            sd((b, NSA_G, t, NSA_P * 3), F32),
        ],
        compiler_params=_cp(("arbitrary", "arbitrary")),
        name="nsa_prep",
    )(z3, z3, z3, z3, qg_t, kg_t, seg_ones)


def _compress_half(load_rows, nch, ph_ref, pe_ref, sh_scr):
    half = NSA_G * NSA_DH
    lo = jnp.zeros((nch, half), F32)
    hi = jnp.zeros((nch, half), F32)
    for s in range(CMP_STRIDE):
        x = load_rows(s)
        lo += _dot((x + pe_ref[s:s + 1, :]).astype(BF16), ph_ref[s])
        hi += _dot((x + pe_ref[CMP_STRIDE + s:CMP_STRIDE + s + 1, :]).astype(BF16),
                   ph_ref[CMP_STRIDE + s])
    sh_scr[0:nch, :] = hi
    sh_scr[nch:nch + 8, :] = jnp.zeros((8, half), F32)
    return lo + sh_scr[1:nch + 1, :]


def _compress_kernel(x_ref, phk_ref, phv_ref, pek_ref, pev_ref, kg_ref, seg_ref,
                     kc_ref, vc_ref, sh_scr):
    nch = x_ref.shape[1]
    half = NSA_G * NSA_DH
    k = _compress_half(lambda s: x_ref[0, :, s * KV_W:s * KV_W + half], nch,
                       phk_ref, pek_ref, sh_scr)
    v = _compress_half(lambda s: x_ref[0, :, s * KV_W + half:(s + 1) * KV_W], nch,
                       phv_ref, pev_ref, sh_scr)
    kn = k * lax.rsqrt(_seg_mean_sq(k, seg_ref[...]) + EPS) * kg_ref[0:1, :]
    for g in range(NSA_G):
        sl = slice(g * NSA_DH, (g + 1) * NSA_DH)
        kc_ref[0, g] = kn[:, sl].astype(BF16)
        vc_ref[0, g] = v[:, sl].astype(BF16)


def _compress_prompt(cmp_rows, phk, phv, pek, pev, kg_t, seg_ones):
    b, nch, _ = cmp_rows.shape
    half = NSA_G * NSA_DH
    c3 = lambda i: (0, 0, 0)
    c2 = lambda i: (0, 0)
    hm = pl.BlockSpec((1, NSA_G, nch, NSA_DH), lambda i: (i, 0, 0, 0))
    return pl.pallas_call(
        _compress_kernel,
        grid=(b,),
        in_specs=[
            pl.BlockSpec((1, nch, CMP_STRIDE * KV_W), lambda i: (i, 0, 0)),
            pl.BlockSpec((CMP_BLOCK, half, half), c3),
            pl.BlockSpec((CMP_BLOCK, half, half), c3),
            pl.BlockSpec((CMP_BLOCK, half), c2),
            pl.BlockSpec((CMP_BLOCK, half), c2),
            pl.BlockSpec((3, half), c2),
            pl.BlockSpec((128, 128), c2),
        ],
        out_specs=[hm, hm],
        out_shape=[jax.ShapeDtypeStruct((b, NSA_G, nch, NSA_DH), BF16)] * 2,
        scratch_shapes=[pltpu.VMEM((nch + 8, half), F32)],
        compiler_params=_cp(("arbitrary",)),
        name="compress_prompt",
    )(cmp_rows, phk, phv, pek, pev, kg_t, seg_ones)


def _gated_heads(o, gates, c):
    qb = o.shape[0] // NSA_P
    parts = []
    for p in range(NSA_P):
        gcol = gates[:, p * 3 + c:p * 3 + c + 1]
        parts.append(o[p * qb:(p + 1) * qb] * gcol)
    return jnp.concatenate(parts, axis=1)


def _rank_select(score_t, n_keep):
    nj = score_t.shape[0]
    jrow = lax.broadcasted_iota(jnp.int32, score_t.shape, 0)
    rank = jnp.zeros(score_t.shape, F32)
    for jp in range(nj):
        r = score_t[jp:jp + 1, :]
        rank = rank + jnp.where(jrow > jp, jnp.where(r >= score_t, 1.0, 0.0),
                                jnp.where(r > score_t, 1.0, 0.0))
    return jnp.where(rank < n_keep, 1.0, 0.0)


def _cmp_attn_kernel(q_ref, kc_ref, vc_ref, gt_ref, sl_ref, m2s_ref,
                     o_ref, sel_ref, *, n_keep):
    i = pl.program_id(2)
    qb = q_ref.shape[2]
    rows = NSA_P * qb
    nc = kc_ref.shape[2]
    q = q_ref[0].reshape(rows, NSA_DH)
    s = _nt(q, kc_ref[0, 0])
    r_i = lax.broadcasted_iota(jnp.int32, (rows, nc), 0)
    q_pos = i * qb + r_i % qb
    c_end = lax.broadcasted_iota(jnp.int32, (rows, nc), 1) * CMP_STRIDE + (CMP_BLOCK - 1)
    mask = c_end <= q_pos
    dist = (q_pos - c_end).astype(F32)
    sc = jnp.where(mask, s - sl_ref[0] * dist, NEG)
    m = jnp.max(sc, axis=-1, keepdims=True)
    e = jnp.where(mask, jnp.exp(sc - m), 0.0)
    p = e / jnp.maximum(jnp.sum(e, axis=-1, keepdims=True), TINY)
    o = _dot(p.astype(BF16), vc_ref[0, 0])
    o_ref[0] = _gated_heads(o, gt_ref[0, 0], 0)
    ps = p[0:qb]
    for pp in range(1, NSA_P):
        ps = ps + p[pp * qb:(pp + 1) * qb]
    hi = ps.astype(BF16)
    lo = (ps - hi.astype(F32)).astype(BF16)
    imp_t = _nt(m2s_ref[...], hi) + _nt(m2s_ref[...], lo)
    nj = imp_t.shape[0]
    jrow = lax.broadcasted_iota(jnp.int32, (nj, qb), 0)
    valid = jrow <= i
    forced = (jrow == 0) | (jrow == i) | (jrow == i - 1)
    score_t = jnp.where(valid, jnp.where(forced, FORCE_SCORE, imp_t), NEG)
    sel_t = _rank_select(score_t, n_keep)
    eye = (lax.broadcasted_iota(jnp.int32, (qb, qb), 0)
           == lax.broadcasted_iota(jnp.int32, (qb, qb), 1)).astype(BF16)
    sel_ref[0, 0] = _nt(eye, sel_t.astype(BF16))


def _cmp_attn_prompt(q_hm, kc, vc, gates, slope_col, m2s_t):
    b, _, t, _ = q_hm.shape
    qb = SEL_BLOCK
    nqb = t // qb
    nc = kc.shape[2]
    nsel = m2s_t.shape[0]
    return pl.pallas_call(
        functools.partial(_cmp_attn_kernel, n_keep=min(TOP_N, nsel)),
        grid=(b, NSA_G, nqb),
        in_specs=[
            pl.BlockSpec((1, NSA_P, qb, NSA_DH), lambda bi, g, i: (bi, g, i, 0)),
            pl.BlockSpec((1, 1, nc, NSA_DH), lambda bi, g, i: (bi, g, 0, 0)),
            pl.BlockSpec((1, 1, nc, NSA_DH), lambda bi, g, i: (bi, g, 0, 0)),
            pl.BlockSpec((1, 1, qb, NSA_P * 3), lambda bi, g, i: (bi, g, i, 0)),
            pl.BlockSpec((1, NSA_P * qb, 1), lambda bi, g, i: (g, 0, 0)),
            pl.BlockSpec((nsel, nc), lambda bi, g, i: (0, 0)),
        ],
        out_specs=[
            pl.BlockSpec((1, qb, NSA_P * NSA_DH), lambda bi, g, i: (bi, i, g)),
            pl.BlockSpec((1, 1, qb, nsel), lambda bi, g, i: (bi, g, i, 0)),
        ],
        out_shape=[
            jax.ShapeDtypeStruct((b, t, D_MODEL), F32),
            jax.ShapeDtypeStruct((b, NSA_G, t, nsel), F32),
        ],
        compiler_params=_cp(("arbitrary", "arbitrary", "arbitrary")),
        name="cmp_attn_prompt",
    )(q_hm, kc, vc, gates, slope_col, m2s_t)


KEY_TILE = 512
WIN_SPAN = WINDOW + 128


def _slc_attn_kernel(q_ref, k_ref, v_ref, gt_ref, sl_ref, sel_ref, ex_ref, o_ref, sc_scr):
    i = pl.program_id(2)
    qb = q_ref.shape[2]
    sel_bias = ((sel_ref[0, 0] - 1.0) * (-NEG)).astype(BF16)
    q_abs = i * qb + lax.broadcasted_iota(jnp.int32, (qb, KEY_TILE), 0)
    lane = lax.broadcasted_iota(jnp.int32, (qb, KEY_TILE), 1)
    lane_row = lax.broadcasted_iota(jnp.int32, (1, KEY_TILE), 1)
    q_all = q_ref[0].reshape(NSA_P * qb, NSA_DH)
    slopes = [sl_ref[0, p * qb:p * qb + 1, :] for p in range(NSA_P)]
    n_tiles = (i * qb) // KEY_TILE + 1

    def scores(kt, ms):
        k0 = pl.multiple_of(kt * KEY_TILE, KEY_TILE)
        kk = k_ref[0, 0, pl.ds(k0, KEY_TILE), :]
        mb = _dot(sel_bias, ex_ref[kt]) + jnp.where(k0 + lane > q_abs, NEG, 0.0)
        rel = (k0 - i * qb + lane_row).astype(F32)
        s_all = _nt(q_all, kk)
        out = []
        for p in range(NSA_P):
            sc = s_all[p * qb:(p + 1) * qb] + (mb + slopes[p] * rel)
            sc_scr[kt, p] = sc
            out.append(jnp.maximum(ms[p], jnp.max(sc, axis=-1, keepdims=True)))
        return tuple(out)

    ms = lax.fori_loop(0, n_tiles, scores,
                       tuple(jnp.full((qb, 1), NEG, F32) for _ in range(NSA_P)))

    def weigh(kt, carry):
        k0 = pl.multiple_of(kt * KEY_TILE, KEY_TILE)
        vv = v_ref[0, 0, pl.ds(k0, KEY_TILE), :]
        lsums, acc = carry
        out, es = [], []
        for p in range(NSA_P):
            e = jnp.exp(sc_scr[kt, p] - ms[p])
            lsum = lsums[p]
            for c in range(KEY_TILE // 128):
                lsum = lsum + e[:, c * 128:(c + 1) * 128]
            out.append(lsum)
            es.append(e.astype(BF16))
        return tuple(out), acc + _dot(jnp.concatenate(es, axis=0), vv)

    init = (tuple(jnp.zeros((qb, 128), F32) for _ in range(NSA_P)),
            jnp.zeros((NSA_P * qb, NSA_DH), F32))
    lsums, acc = lax.fori_loop(0, n_tiles, weigh, init)
    l = jnp.concatenate([jnp.sum(ls, axis=-1, keepdims=True) for ls in lsums], axis=0)
    o_ref[0] = _gated_heads(acc / jnp.maximum(l, TINY), gt_ref[0, 0], 1)


def _win_attn_kernel(q_ref, k_ref, v_ref, gt_ref, sl_ref, o_ref):
    i = pl.program_id(2)
    qb = q_ref.shape[2]
    k0 = pl.multiple_of((jnp.maximum(i * qb - WINDOW, 0) // 128) * 128, 128)
    kk = k_ref[0, 0, pl.ds(k0, WIN_SPAN), :]
    vv = v_ref[0, 0, pl.ds(k0, WIN_SPAN), :]
    q_abs = i * qb + lax.broadcasted_iota(jnp.int32, (qb, WIN_SPAN), 0)
    dist = q_abs - (k0 + lax.broadcasted_iota(jnp.int32, (qb, WIN_SPAN), 1))
    mb = jnp.where((dist >= 0) & (dist <= WINDOW), 0.0, NEG)
    rel = (k0 - i * qb + lax.broadcasted_iota(jnp.int32, (1, WIN_SPAN), 1)).astype(F32)
    s_all = _nt(q_ref[0].reshape(NSA_P * qb, NSA_DH), kk)
    es, ls = [], []
    for p in range(NSA_P):
        sc = s_all[p * qb:(p + 1) * qb] + (mb + sl_ref[0, p * qb:p * qb + 1, :] * rel)
        m = jnp.max(sc, axis=-1, keepdims=True)
        e = jnp.exp(sc - m)
        ls.append(jnp.sum(e, axis=-1, keepdims=True))
        es.append(e.astype(BF16))
    o = _dot(jnp.concatenate(es, axis=0), vv) / jnp.maximum(jnp.concatenate(ls, axis=0), TINY)
    o_ref[0] = _gated_heads(o, gt_ref[0, 0], 2)


def _branch_attn_prompt(q_hm, k_hm, v_hm, gates, slope_col, sel=None, expand=None):
    b, _, t, _ = q_hm.shape
    qb = SEL_BLOCK
    nqb = t // qb
    in_specs = [
        pl.BlockSpec((1, NSA_P, qb, NSA_DH), lambda bi, g, i: (bi, g, i, 0)),
        pl.BlockSpec((1, 1, t, NSA_DH), lambda bi, g, i: (bi, g, 0, 0)),
        pl.BlockSpec((1, 1, t, NSA_DH), lambda bi, g, i: (bi, g, 0, 0)),
        pl.BlockSpec((1, 1, qb, NSA_P * 3), lambda bi, g, i: (bi, g, i, 0)),
        pl.BlockSpec((1, NSA_P * qb, 1), lambda bi, g, i: (g, 0, 0)),
    ]
    args = [q_hm, k_hm, v_hm, gates, slope_col]
    scratch = []
    if sel is None:
        body, name = _win_attn_kernel, "attn_win"
    else:
        scratch = [pltpu.VMEM((t // KEY_TILE, NSA_P, qb, KEY_TILE), F32)]
        nsel = sel.shape[-1]
        in_specs += [
            pl.BlockSpec((1, 1, qb, nsel), lambda bi, g, i: (bi, g, i, 0)),
            pl.BlockSpec((expand.shape[0], nsel, KEY_TILE), lambda bi, g, i: (0, 0, 0)),
        ]
        args += [sel, expand]
        body, name = _slc_attn_kernel, "attn_slc"
    return pl.pallas_call(
        body,
        grid=(b, NSA_G, nqb),
        in_specs=in_specs,
        out_specs=pl.BlockSpec((1, qb, NSA_P * NSA_DH), lambda bi, g, i: (bi, i, g)),
        out_shape=jax.ShapeDtypeStruct((b, t, D_MODEL), F32),
        scratch_shapes=scratch,
        compiler_params=_cp(("arbitrary", "arbitrary", "arbitrary")),
        name=name,
    )(*args)


def _merge_kernel(x_ref, oa_ref, ob_ref, c1_ref, c2_ref, c3_ref, ma_ref, mb_ref, mc_ref,
                  wa_ref, wb_ref, wc_ref, wo_ref, o_ref):
    oc = c1_ref[...] + c2_ref[...] + c3_ref[...]
    merged = (jax.nn.sigmoid(ma_ref[...]) * _dot(oa_ref[...].astype(BF16), wa_ref[...])
              + jax.nn.sigmoid(mb_ref[...]) * _dot(ob_ref[...].astype(BF16), wb_ref[...])
              + jax.nn.sigmoid(mc_ref[...]) * _dot(oc.astype(BF16), wc_ref[...]))
    o_ref[...] = x_ref[...] + _dot(merged.astype(BF16), wo_ref[...])


def _merge(x2d, z2d, oa, ob, c1, c2, c3, wa, wb, wc, wo):
    n = x2d.shape[0]
    tt = min(n, 256)
    tok = pl.BlockSpec((tt, D_MODEL), lambda i: (i, 0))
    wsp = pl.BlockSpec((D_MODEL, D_MODEL), lambda i: (0, 0))
    zsp = [pl.BlockSpec((tt, D_MODEL), functools.partial(lambda i, c: (i, c), c=C_MGATE // D_MODEL + c))
           for c in range(3)]
    return pl.pallas_call(
        _merge_kernel,
        grid=(n // tt,),
        in_specs=[tok] * 6 + zsp + [wsp] * 4,
        out_specs=tok,
        out_shape=jax.ShapeDtypeStruct((n, D_MODEL), F32),
        compiler_params=_cp(("arbitrary",)),
        name="merge",
    )(x2d, oa, ob, c1, c2, c3, z2d, z2d, z2d, wa, wb, wc, wo)


FF_TILE = 1408


def _ffn_kernel(x_ref, g_ref, wg_ref, wu_ref, wd_ref, o_ref, h_scr, acc_scr):
    j = pl.program_id(1)

    @pl.when(j == 0)
    def _():
        x = x_ref[...]
        ms = jnp.mean(x * x, axis=-1, keepdims=True)
        h_scr[...] = (x * lax.rsqrt(ms + EPS) * g_ref[...]).astype(BF16)
        acc_scr[...] = x

    h = h_scr[...]
    act = _silu(_dot(h, wg_ref[...])) * _dot(h, wu_ref[...])
    acc_scr[...] += _dot(act.astype(BF16), wd_ref[...])

    @pl.when(j == pl.num_programs(1) - 1)
    def _():
        o_ref[...] = acc_scr[...]


def _ffn(x2d, g, wg, wu, wd):
    n = x2d.shape[0]
    tt = min(n, 512)
    return pl.pallas_call(
        _ffn_kernel,
        grid=(n // tt, D_FF // FF_TILE),
        in_specs=[
            pl.BlockSpec((tt, D_MODEL), lambda i, j: (i, 0)),
            pl.BlockSpec((1, D_MODEL), lambda i, j: (0, 0)),
            pl.BlockSpec((D_MODEL, FF_TILE), lambda i, j: (0, j)),
            pl.BlockSpec((D_MODEL, FF_TILE), lambda i, j: (0, j)),
            pl.BlockSpec((FF_TILE, D_MODEL), lambda i, j: (j, 0)),
        ],
        out_specs=pl.BlockSpec((tt, D_MODEL), lambda i, j: (i, 0)),
        out_shape=jax.ShapeDtypeStruct((n, D_MODEL), F32),
        scratch_shapes=[pltpu.VMEM((tt, D_MODEL), BF16), pltpu.VMEM((tt, D_MODEL), F32)],
        compiler_params=_cp(("arbitrary", "arbitrary")),
        name="ffn",
    )(x2d, g, wg, wu, wd)


IN_SIZES = (1024, 1024, 512, 512, 1024, 1024, 16, 1024, 512, 512, 512, 48, 3072)


def _perm_w_in(w):
    parts, start = [], 0
    for n in IN_SIZES:
        parts.append(w[:, start:start + n])
        start += n
    (rg_y, rg_x, g_q, g_k, g_v, g_r, g_a, n_q, n_cmp, n_slc, n_win, n_gate, m_gate) = parts
    pad = jnp.zeros((w.shape[0], ZC - C_SMALL - GLA_RANK - 3 * NSA_HEADS), w.dtype)
    return jnp.concatenate([m_gate, rg_y, rg_x, g_q, g_k, g_v, g_r, n_q, n_cmp, n_slc, n_win,
                            g_a, n_gate, pad], axis=1).astype(BF16)


def _block_diag4(phi):
    eye = jnp.eye(NSA_G, dtype=phi.dtype)
    out = jnp.einsum('gh,sde->sgdhe', eye, phi)
    return out.reshape(phi.shape[0], NSA_G * NSA_DH, NSA_G * NSA_DH)


def _alibi_slopes():
    h = jnp.arange(1, NSA_HEADS + 1, dtype=F32)
    return (2.0 ** (-8.0 * h / NSA_HEADS)).reshape(NSA_G, NSA_P)


def _layer_params(l, norm_mix_g, w_in, rg_conv_w, rg_conv_b, rg_w_a, rg_b_a, rg_w_x, rg_b_x,
                  rg_lambda, gla_w_a, gla_b_a, gla_norm_g, nsa_q_norm_g, nsa_k_norm_g, nsa_phi_k,
                  nsa_phi_v, nsa_cmp_pe, w_rg_out, w_gla_out, w_nsa_out, w_out, norm_ffn_g,
                  w_ffn_gate, w_ffn_up, w_ffn_down):
    half = NSA_G * NSA_DH
    p = {}
    p['norm_mix'] = norm_mix_g[l][None, :]
    p['w_in'] = _perm_w_in(w_in[l])
    p['cw'] = rg_conv_w[l]
    p['cb'] = rg_conv_b[l][None, :]
    p['wax'] = jnp.concatenate([rg_w_a[l], rg_w_x[l]], axis=-1).astype(BF16)
    p['ba'] = rg_b_a[l][None, :]
    p['bx'] = rg_b_x[l][None, :]
    p['lam'] = rg_lambda[l][None, :]
    p['gla_wa'] = jnp.pad(gla_w_a[l], ((0, SMALL_W - GLA_RANK), (0, 0))).astype(BF16)
    p['gla_ba'] = gla_b_a[l][None, :]
    p['gla_gn'] = gla_norm_g[l][None, :]
    p['qg'] = jnp.tile(nsa_q_norm_g[l], NSA_HEADS)[None, :]
    p['kg'] = jnp.tile(nsa_k_norm_g[l], (1, NSA_G))
    p['phk'] = _block_diag4(nsa_phi_k[l]).astype(BF16)
    p['phv'] = _block_diag4(nsa_phi_v[l]).astype(BF16)
    p['pek'] = jnp.tile(nsa_cmp_pe[l, 0], (1, NSA_G))
    p['pev'] = jnp.tile(nsa_cmp_pe[l, 1], (1, NSA_G))
    p['w_rg_out'] = w_rg_out[l].astype(BF16)
    p['w_gla_out'] = w_gla_out[l].astype(BF16)
    p['w_nsa_out'] = w_nsa_out[l].astype(BF16)
    p['w_out'] = w_out[l].astype(BF16)
    p['norm_ffn'] = norm_ffn_g[l][None, :]
    p['wg'] = w_ffn_gate[l].astype(BF16)
    p['wu'] = w_ffn_up[l].astype(BF16)
    p['wd'] = w_ffn_down[l].astype(BF16)
    del half
    return p


def _seg_ones():
    r = jnp.arange(128) // NSA_DH
    return (r[:, None] == r[None, :]).astype(BF16)


def _cmp_to_sel_t(n_cmp_rows, n_cmp_valid, n_sel):
    start = jnp.arange(n_cmp_rows)[None, :] * CMP_STRIDE
    sel = jnp.arange(n_sel)[:, None] * SEL_BLOCK
    ok = (start < sel + SEL_BLOCK) & (start + CMP_BLOCK > sel) & (jnp.arange(n_cmp_rows)[None, :] < n_cmp_valid)
    return ok.astype(BF16)


def _expand_blocks(n_sel, n_keys):
    key_blk = jnp.arange(n_keys) // SEL_BLOCK
    e = (jnp.arange(n_sel)[:, None] == key_blk[None, :]).astype(BF16)
    return e.reshape(n_sel, n_keys // KEY_TILE, KEY_TILE).transpose(1, 0, 2)


def _slope_cols(rows_per_head):
    return jnp.repeat(_alibi_slopes(), rows_per_head, axis=1)[:, :, None]


def _prompt_layer(x, p):
    b, t, _ = x.shape
    n = b * t
    z2 = _inproj(x.reshape(n, D_MODEL), p['norm_mix'], p['w_in'])
    z3 = z2.reshape(b, t, ZC)
    o_a, h_last = _rglru_prompt(z3, p['cw'], p['cb'], p['wax'], p['ba'], p['bx'], p['lam'])
    o_b, st = _gla_prompt(z3, p['gla_wa'], p['gla_ba'], p['gla_gn'])
    seg = _seg_ones()
    q_hm, slc_n, ks, vs, win_n, kw, vw, gates = _nsa_prep(z3, p['qg'], p['kg'], seg)
    nch = t // CMP_STRIDE
    cmp_flat = z3[:, :, C_NCMP:C_NCMP + KV_W]
    kc, vc = _compress_prompt(cmp_flat.reshape(b, nch, CMP_STRIDE * KV_W), p['phk'], p['phv'],
                              p['pek'], p['pev'], p['kg'], seg)
    n_sel = t // SEL_BLOCK
    m2s_t = _cmp_to_sel_t(nch, nch - 1, n_sel)
    slope_col = _slope_cols(SEL_BLOCK)
    o_cmp, sel = _cmp_attn_prompt(q_hm, kc, vc, gates, slope_col, m2s_t)
    expand = _expand_blocks(n_sel, t)
    o_slc = _branch_attn_prompt(q_hm, ks, vs, gates, slope_col, sel, expand)
    o_win = _branch_attn_prompt(q_hm, kw, vw, gates, slope_col)
    x2 = _merge(x.reshape(n, D_MODEL), z2, o_a.reshape(n, D_MODEL), o_b.reshape(n, D_MODEL),
                o_cmp.reshape(n, D_MODEL), o_slc.reshape(n, D_MODEL), o_win.reshape(n, D_MODEL),
                p['w_rg_out'], p['w_gla_out'], p['w_nsa_out'], p['w_out'])
    y = _ffn(x2, p['norm_ffn'], p['wg'], p['wu'], p['wd']).reshape(b, t, D_MODEL)
    kv_shape = (b, t, 2, NSA_G, NSA_DH)
    cmp_kv = cmp_flat.reshape(kv_shape)
    slc_kv = slc_n.reshape(kv_shape)
    win_state = win_n[:, t - min(WINDOW, t):].reshape(b, min(WINDOW, t), 2, NSA_G, NSA_DH)
    s_new = jnp.swapaxes(st, -1, -2)
    conv_new = z3[:, t - (RG_CONV_W - 1):, C_RGX:C_RGX + D_MODEL]
    return y, (cmp_kv, slc_kv, win_state, s_new, h_last[:, 7], conv_new)


def _rglru_step_kernel(y_ref, x_ref, c0_ref, c1_ref, c2_ref, h0_ref, cw_ref, cb_ref, wax_ref,
                       ba_ref, bx_ref, lam_ref, o_ref, h_ref):
    cw = cw_ref[...]
    x = x_ref[...]
    xc = (cb_ref[...] + c0_ref[...] * cw[0:1] + c1_ref[...] * cw[1:2]
          + c2_ref[...] * cw[2:3] + x * cw[3:4])
    a_parts, u_parts = _rglru_gates(xc, wax_ref, ba_ref[...], bx_ref[...], lam_ref[...])
    a = jnp.concatenate(a_parts, axis=1)
    u = jnp.concatenate(u_parts, axis=1)
    h = u + a * h0_ref[...]
    h_ref[...] = h
    o_ref[...] = _gelu_tanh(y_ref[...]) * h


def _rglru_sample(z2, conv_state, h0, cw, cb, wax, ba, bx, lam):
    n = z2.shape[0]
    tok = lambda c: pl.BlockSpec((n, D_MODEL), functools.partial(lambda i, c: (0, c), c=c))
    full = pl.BlockSpec((n, D_MODEL), lambda i: (0, 0))
    row = pl.BlockSpec((1, D_MODEL), lambda i: (0, 0))
    return pl.pallas_call(
        _rglru_step_kernel,
        grid=(1,),
        in_specs=[tok(C_RGY // D_MODEL), tok(C_RGX // D_MODEL), full, full, full, full,
                  pl.BlockSpec((RG_CONV_W, D_MODEL), lambda i: (0, 0)), row,
                  pl.BlockSpec((RG_BLOCKS, RG_BLOCK_W, 2 * RG_BLOCK_W), lambda i: (0, 0, 0)),
                  row, row, row],
        out_specs=[full, full],
        out_shape=[jax.ShapeDtypeStruct((n, D_MODEL), F32)] * 2,
        compiler_params=_cp(("arbitrary",)),
        name="rglru_sample",
    )(z2, z2, conv_state[:, 0], conv_state[:, 1], conv_state[:, 2], h0, cw, cb, wax, ba, bx, lam)


def _row_to_col(row):
    n = row.shape[1]
    eye = lax.broadcasted_iota(jnp.int32, (n, n), 0) == lax.broadcasted_iota(jnp.int32, (n, n), 1)
    return jnp.sum(jnp.where(eye, jnp.broadcast_to(row, (n, n)), 0.0), axis=1, keepdims=True)


def _gla_step_kernel(q_ref, k_ref, v_ref, r_ref, ga_ref, s_ref, wa_ref, ba_ref, gn_ref,
                     o_ref, so_ref):
    ga = jnp.broadcast_to(ga_ref[0], (8, SMALL_W)).astype(BF16)
    xg = _dot(ga, wa_ref[...])[0:1] + ba_ref[...]
    lg = _log_sigmoid(xg) * (1.0 / GLA_TAU)
    for h in range(GLA_HEADS):
        ksl = slice(h * GLA_DK, (h + 1) * GLA_DK)
        vsl = slice(h * GLA_DV, (h + 1) * GLA_DV)
        q = q_ref[0, :, ksl] * (GLA_DK ** -0.5)
        k = k_ref[0, :, ksl]
        v = v_ref[0, :, vsl]
        b = lg[:, ksl]
        eb = jnp.exp(b)
        s_old = s_ref[0, h]
        a = jnp.sum(q * k, axis=-1, keepdims=True)
        qe = jnp.broadcast_to(q * eb, (8, GLA_DK)).astype(BF16)
        o = a * v + _dot(qe, s_old.astype(BF16))[0:1]
        so_ref[0, h] = _row_to_col(eb) * s_old + _row_to_col(k) * v
        ms = jnp.mean(o * o, axis=-1, keepdims=True)
        on = o * lax.rsqrt(ms + EPS) * gn_ref[...]
        o_ref[0, :, vsl] = on * _silu(r_ref[0, :, vsl])


def _gla_sample(z3, state, wa_pad, ba, gn):
    n = z3.shape[0]
    hk = GLA_HEADS * GLA_DK
    hv = GLA_HEADS * GLA_DV
    row = lambda i: (0, 0)
    st = pl.BlockSpec((1, GLA_HEADS, GLA_DK, GLA_DV), lambda i: (i, 0, 0, 0))
    return pl.pallas_call(
        _gla_step_kernel,
        grid=(n,),
        in_specs=[
            pl.BlockSpec((1, 1, hk), lambda i: (i, 0, C_GQ // hk)),
            pl.BlockSpec((1, 1, hk), lambda i: (i, 0, C_GK // hk)),
            pl.BlockSpec((1, 1, hv), lambda i: (i, 0, C_GV // hv)),
            pl.BlockSpec((1, 1, hv), lambda i: (i, 0, C_GR // hv)),
            pl.BlockSpec((1, 1, SMALL_W), lambda i: (i, 0, C_SMALL // SMALL_W)),
            st,
            pl.BlockSpec((SMALL_W, hk), row),
            pl.BlockSpec((1, hk), row),
            pl.BlockSpec((1, GLA_DV), row),
        ],
        out_specs=[pl.BlockSpec((1, 1, hv), lambda i: (i, 0, 0)), st],
        out_shape=[jax.ShapeDtypeStruct((n, 1, hv), F32),
                   jax.ShapeDtypeStruct(state.shape, F32)],
        compiler_params=_cp(("arbitrary",)),
        name="gla_sample",
    )(z3, z3, z3, z3, z3, state, wa_pad, ba, gn)


def _group_fold(o):
    grp = lax.broadcasted_iota(jnp.int32, (NSA_HEADS, NSA_DH), 0) // NSA_P
    out = jnp.zeros((NSA_HEADS, NSA_DH), F32)
    for g in range(NSA_G):
        out = out + jnp.where(grp == g, o[:, g * NSA_DH:(g + 1) * NSA_DH], 0.0)
    return out


def _nsa_sample_kernel(pt_ref, q_ref, slc_new_ref, win_new_ref, sm_ref, wb_ref, sl_ref,
                       phk_ref, phv_ref, pek_ref, pev_ref, kg_ref, seg_ref, m2s_ref,
                       cmp_pool, slc_pool, o_ref, wo_ref, buf, xs_scr, sh_scr, sc_scr, sem,
                       *, layer_off, n_keep):
    bi = pl.program_id(0)
    n_pages = pt_ref.shape[1]
    past = n_pages * PAGE_SIZE
    half = NSA_G * NSA_DH
    nch = past // CMP_STRIDE
    cpp = PAGE_SIZE // CMP_STRIDE

    def page_copy(pool, pg):
        return pltpu.make_async_copy(pool.at[layer_off + pt_ref[bi, pg]], buf.at[pg], sem)

    def gather_start(pool):
        def start(pg, c):
            page_copy(pool, pg).start()
            return c

        lax.fori_loop(0, n_pages, start, 0)

    def gather_wait(pool):
        def wait(pg, c):
            page_copy(pool, pg).wait()
            return c

        lax.fori_loop(0, n_pages, wait, 0)

    r_i = lax.broadcasted_iota(jnp.int32, (PAGE_SIZE, PAGE_SIZE), 0)
    p_i = lax.broadcasted_iota(jnp.int32, (PAGE_SIZE, PAGE_SIZE), 1)
    perm = (p_i == CMP_STRIDE * (r_i % cpp) + r_i // cpp).astype(BF16)

    def compress_pool_half(row0, ph_ref, pe_ref):
        def fill(pg, c):
            x = buf[pg, row0:row0 + half, :]
            hi = x.astype(BF16)
            lo = (x - hi.astype(F32)).astype(BF16)
            xt = _nt(perm, hi) + _nt(perm, lo)
            r0 = pl.multiple_of(pg * cpp, cpp)
            for s in range(CMP_STRIDE):
                xs_scr[s, pl.ds(r0, cpp), :] = xt[s * cpp:(s + 1) * cpp]
            return c

        lax.fori_loop(0, n_pages, fill, 0)
        return _compress_half(lambda s: xs_scr[s], nch, ph_ref, pe_ref, sh_scr)

    q16 = q_ref[0]
    head = lax.broadcasted_iota(jnp.int32, (NSA_HEADS, half), 0)
    lane = lax.broadcasted_iota(jnp.int32, (NSA_HEADS, half), 1)
    qblk = jnp.where(lane // NSA_DH == head // NSA_P,
                     jnp.concatenate([q16] * NSA_G, axis=1), jnp.zeros((), BF16))
    slope = sl_ref[...]
    sig = jax.nn.sigmoid(sm_ref[0])
    lane_s = lax.broadcasted_iota(jnp.int32, (NSA_HEADS, SMALL_W), 1)
    head_s = lax.broadcasted_iota(jnp.int32, (NSA_HEADS, SMALL_W), 0)

    def gate_col(c):
        return jnp.sum(jnp.where(lane_s == GATE_OFF + 3 * head_s + c, sig, 0.0),
                       axis=1, keepdims=True)

    gather_start(cmp_pool)
    gather_wait(cmp_pool)
    kc = compress_pool_half(0, phk_ref, pek_ref)
    vc = compress_pool_half(half, phv_ref, pev_ref)
    gather_start(slc_pool)
    kn = kc * lax.rsqrt(_seg_mean_sq(kc, seg_ref[...]) + EPS) * kg_ref[0:1, :]
    s = _nt(qblk, kn.astype(BF16))
    c_end = lax.broadcasted_iota(jnp.int32, (NSA_HEADS, nch), 1) * CMP_STRIDE + (CMP_BLOCK - 1)
    mask = c_end <= past
    sc = jnp.where(mask, s - slope * (past - c_end).astype(F32), NEG)
    m = jnp.max(sc, axis=-1, keepdims=True)
    e = jnp.where(mask, jnp.exp(sc - m), 0.0)
    p = e / jnp.maximum(jnp.sum(e, axis=-1, keepdims=True), TINY)
    o_cmp = _group_fold(_dot(p.astype(BF16), vc.astype(BF16)))
    ps_rows = []
    for g in range(NSA_G):
        r = p[g * NSA_P:g * NSA_P + 1]
        for pp in range(1, NSA_P):
            r = r + p[g * NSA_P + pp:g * NSA_P + pp + 1]
        ps_rows.append(r)
    ps = jnp.concatenate(ps_rows + [jnp.zeros((8 - NSA_G, nch), F32)], axis=0)
    hi = ps.astype(BF16)
    lo = (ps - hi.astype(F32)).astype(BF16)
    imp = _dot(hi, m2s_ref[...]) + _dot(lo, m2s_ref[...])
    nselp = imp.shape[1]
    cur = past // SEL_BLOCK
    jl = lax.broadcasted_iota(jnp.int32, (8, nselp), 1)
    forced = (jl == 0) | (jl == cur) | (jl == cur - 1)
    score = jnp.where(jl <= cur, jnp.where(forced, FORCE_SCORE, imp), NEG)
    jr = lax.broadcasted_iota(jnp.int32, (nselp, nselp), 0)
    jc = lax.broadcasted_iota(jnp.int32, (nselp, nselp), 1)
    sel_rows = []
    for g in range(NSA_G):
        row = score[g:g + 1]
        col = _row_to_col(row)
        ahead = jnp.where(jc > jr, jnp.where(col >= row, 1.0, 0.0), jnp.where(col > row, 1.0, 0.0))
        rank = jnp.sum(ahead, axis=0, keepdims=True)
        sel_g = jnp.where(rank < n_keep, 1.0, 0.0)
        sel_rows += [sel_g] * NSA_P
    sel16 = jnp.concatenate(sel_rows, axis=0).astype(BF16)

    qf = qblk.astype(F32)

    def self_score(row):
        return jnp.sum(qf * row[:, 0:half].astype(BF16).astype(F32), axis=-1, keepdims=True)

    def add_self(m, l, acc, s_new, row):
        v_new = row[:, half:2 * half].astype(BF16).astype(F32)
        m_new = jnp.maximum(m, s_new)
        alpha = jnp.exp(m - m_new)
        e_new = jnp.exp(s_new - m_new)
        l = alpha * l + e_new
        acc = alpha * acc + e_new.astype(BF16).astype(F32) * v_new
        return l, acc

    gather_wait(slc_pool)
    lane_p = lax.broadcasted_iota(jnp.int32, (NSA_HEADS, PAGE_SIZE), 1)
    j_i = lax.broadcasted_iota(jnp.int32, (nselp, PAGE_SIZE), 0)
    blk_in_page = lax.broadcasted_iota(jnp.int32, (nselp, PAGE_SIZE), 1) // SEL_BLOCK
    bpp = PAGE_SIZE // SEL_BLOCK

    def score(pg, c):
        s = _dot(qblk, buf[pg, 0:half, :].astype(BF16))
        in_page = (j_i == pg * bpp + blk_in_page).astype(BF16)
        picked = _dot(sel16, in_page) > 0.5
        dist = (past - (pg * PAGE_SIZE + lane_p)).astype(F32)
        sc_scr[pg] = jnp.where(picked, s - slope * dist, NEG)
        return c

    lax.fori_loop(0, n_pages, score, 0)
    s_self = self_score(slc_new_ref[0])
    sc_all = sc_scr[...]
    m = jnp.max(jnp.max(sc_all, axis=0), axis=-1, keepdims=True)
    m_all = jnp.maximum(m, s_self)
    e_all = jnp.exp(sc_all - m_all[None])
    l = jnp.sum(jnp.sum(e_all, axis=0), axis=-1, keepdims=True)
    sc_scr[...] = e_all

    def weigh(pg, acc):
        return acc + _nt(sc_scr[pg].astype(BF16), buf[pg, half:2 * half, :].astype(BF16))

    acc = lax.fori_loop(0, n_pages, weigh, jnp.zeros((NSA_HEADS, half), F32))
    l, acc = add_self(m_all, l, acc, s_self, slc_new_ref[0])
    o_slc = _group_fold(acc / jnp.maximum(l, TINY))

    w_len = wb_ref.shape[1]
    kk = wb_ref[0, :, 0:half].astype(BF16)
    vv = wb_ref[0, :, half:2 * half].astype(BF16)
    s = _nt(qblk, kk)
    dist = w_len - lax.broadcasted_iota(jnp.int32, (NSA_HEADS, w_len), 1)
    mask = dist <= WINDOW
    sc = jnp.where(mask, s - slope * dist.astype(F32), NEG)
    m = jnp.max(sc, axis=-1, keepdims=True)
    e = jnp.where(mask, jnp.exp(sc - m), 0.0)
    l = jnp.sum(e, axis=-1, keepdims=True)
    acc = _dot(e.astype(BF16), vv)
    l, acc = add_self(m, l, acc, self_score(win_new_ref[0]), win_new_ref[0])
    o_win = _group_fold(acc / jnp.maximum(l, TINY))

    o_ref[0] = gate_col(0) * o_cmp + gate_col(1) * o_slc + gate_col(2) * o_win
    wo_ref[0, 0:w_len - 1, :] = wb_ref[0, 1:w_len, :]
    wo_ref[0, w_len - 1:w_len, :] = win_new_ref[0]


def _nsa_sample(page_table, q_bhd, slc_new, win_new, z3, win_buf, cmp_pool, slc_pool, layer_off,
                p, seg):
    n, n_pages = page_table.shape
    past = n_pages * PAGE_SIZE
    half = NSA_G * NSA_DH
    nch = past // CMP_STRIDE
    n_sel = past // SEL_BLOCK + 1
    nselp = -(-n_sel // 128) * 128
    w_len = win_buf.shape[1]
    m2s = jnp.pad(_cmp_to_sel_t(nch, nch - 1, n_sel).T, ((0, 0), (0, nselp - n_sel)))
    slope16 = _alibi_slopes().reshape(NSA_HEADS, 1)
    c2 = lambda i, pt: (0, 0)
    c3 = lambda i, pt: (0, 0, 0)
    once = pl.Buffered(1)
    grid_spec = pltpu.PrefetchScalarGridSpec(
        num_scalar_prefetch=1,
        grid=(n,),
        in_specs=[
            pl.BlockSpec((1, NSA_HEADS, NSA_DH), lambda i, pt: (i, 0, 0)),
            pl.BlockSpec((1, 1, KV_W), lambda i, pt: (i, 0, 0)),
            pl.BlockSpec((1, 1, KV_W), lambda i, pt: (i, 0, 0)),
            pl.BlockSpec((1, 1, SMALL_W), lambda i, pt: (i, 0, C_SMALL // SMALL_W)),
            pl.BlockSpec((1, w_len, KV_W), lambda i, pt: (i, 0, 0)),
            pl.BlockSpec((NSA_HEADS, 1), c2),
            pl.BlockSpec((CMP_BLOCK, half, half), c3, pipeline_mode=once),
            pl.BlockSpec((CMP_BLOCK, half, half), c3, pipeline_mode=once),
            pl.BlockSpec((CMP_BLOCK, half), c2),
            pl.BlockSpec((CMP_BLOCK, half), c2),
            pl.BlockSpec((3, half), c2),
            pl.BlockSpec((128, 128), c2),
            pl.BlockSpec((nch, nselp), c2),
            pl.BlockSpec(memory_space=pl.ANY),
            pl.BlockSpec(memory_space=pl.ANY),
        ],
        out_specs=[
            pl.BlockSpec((1, NSA_HEADS, NSA_DH), lambda i, pt: (i, 0, 0)),
            pl.BlockSpec((1, w_len, KV_W), lambda i, pt: (i, 0, 0)),
        ],
        scratch_shapes=[
            pltpu.VMEM((n_pages, KV_W, PAGE_SIZE), F32),
            pltpu.VMEM((CMP_STRIDE, nch, half), F32),
            pltpu.VMEM((nch + 8, half), F32),
            pltpu.VMEM((n_pages, NSA_HEADS, PAGE_SIZE), F32),
            pltpu.SemaphoreType.DMA(()),
        ],
    )
    return pl.pallas_call(
        functools.partial(_nsa_sample_kernel, layer_off=layer_off, n_keep=min(TOP_N, n_sel)),
        grid_spec=grid_spec,
        out_shape=[jax.ShapeDtypeStruct((n, NSA_HEADS, NSA_DH), F32),
                   jax.ShapeDtypeStruct((n, w_len, KV_W), F32)],
        compiler_params=_cp(("arbitrary",)),
        name="nsa_sample",
    )(page_table, q_bhd, slc_new, win_new, z3, win_buf, slope16, p['phk'], p['phv'], p['pek'],
      p['pev'], p['kg'], seg, m2s, cmp_pool, slc_pool)


def _sample_layer(x, p, l, cmp_pool, slc_pool, win_buf, state_gla, state_rglru, state_conv,
                  page_table, n_pool):
    n = x.shape[0]
    xs = x.reshape(n, D_MODEL)
    z2 = _inproj(xs, p['norm_mix'], p['w_in'])
    z3 = z2.reshape(n, 1, ZC)
    o_a, h_new = _rglru_sample(z2, state_conv, state_rglru, p['cw'], p['cb'], p['wax'], p['ba'],
                               p['bx'], p['lam'])
    o_b, s_new = _gla_sample(z3, state_gla, p['gla_wa'], p['gla_ba'], p['gla_gn'])
    seg = _seg_ones()
    q_hm, slc_n, _, _, win_n, _, _, _ = _nsa_prep(z2.reshape(1, n, ZC), p['qg'], p['kg'], seg)
    q_bhd = jnp.swapaxes(q_hm[0], 0, 1)
    slc_new = slc_n.reshape(n, 1, KV_W)
    win_new = win_n.reshape(n, 1, KV_W)
    w_len = win_buf.shape[1]
    o_c, win_state = _nsa_sample(page_table, q_bhd, slc_new, win_new, z3,
                                 win_buf.reshape(n, w_len, KV_W), cmp_pool, slc_pool, l * n_pool,
                                 p, seg)
    zero = jnp.zeros((n, D_MODEL), F32)
    x2 = _merge(xs, z2, o_a, o_b.reshape(n, D_MODEL), o_c.reshape(n, D_MODEL), zero, zero,
                p['w_rg_out'], p['w_gla_out'], p['w_nsa_out'], p['w_out'])
    y = _ffn(x2, p['norm_ffn'], p['wg'], p['wu'], p['wd']).reshape(n, 1, D_MODEL)
    kv_shape = (n, 1, 2, NSA_G, NSA_DH)
    cmp_kv = z2[:, C_NCMP:C_NCMP + KV_W].reshape(kv_shape)
    conv_new = jnp.concatenate([state_conv[:, 1:], z2[:, None, C_RGX:C_RGX + D_MODEL]], axis=1)
    return y, (cmp_kv, slc_n.reshape(kv_shape), win_state.reshape(n, w_len, 2, NSA_G, NSA_DH),
               s_new, h_new, conv_new)


def kernel(x_prompt, x_sample, cache_cmp_kv, cache_slc_kv, cache_win_kv, state_gla, state_rglru,
           state_conv, page_table, norm_mix_g, w_in, rg_conv_w, rg_conv_b, rg_w_a, rg_b_a, rg_w_x,
           rg_b_x, rg_lambda, gla_w_a, gla_b_a, gla_norm_g, nsa_q_norm_g, nsa_k_norm_g, nsa_phi_k,
           nsa_phi_v, nsa_cmp_pe, w_rg_out, w_gla_out, w_nsa_out, w_out, norm_ffn_g, w_ffn_gate,
           w_ffn_up, w_ffn_down):
    depth, n_pool = cache_cmp_kv.shape[:2]
    pool_shape = (depth * n_pool, KV_W, PAGE_SIZE)
    cmp_pool = jnp.transpose(cache_cmp_kv, (0, 1, 3, 4, 5, 2)).reshape(pool_shape)
    slc_pool = jnp.transpose(cache_slc_kv, (0, 1, 3, 4, 5, 2)).reshape(pool_shape)
    yp, ys = x_prompt, x_sample
    p_out = [[] for _ in range(6)]
    s_out = [[] for _ in range(6)]
    for l in range(depth):
        p = _layer_params(l, norm_mix_g, w_in, rg_conv_w, rg_conv_b, rg_w_a, rg_b_a, rg_w_x,
                          rg_b_x, rg_lambda, gla_w_a, gla_b_a, gla_norm_g, nsa_q_norm_g,
                          nsa_k_norm_g, nsa_phi_k, nsa_phi_v, nsa_cmp_pe, w_rg_out, w_gla_out,
                          w_nsa_out, w_out, norm_ffn_g, w_ffn_gate, w_ffn_up, w_ffn_down)
        yp, p_new = _prompt_layer(yp, p)
        ys, s_new = _sample_layer(ys, p, l, cmp_pool, slc_pool, cache_win_kv[l], state_gla[l],
                                  state_rglru[l], state_conv[l], page_table, n_pool)
        for i in range(6):
            p_out[i].append(p_new[i])
            s_out[i].append(s_new[i])
    return ((yp, ys) + tuple(jnp.stack(a) for a in p_out) + tuple(jnp.stack(a) for a in s_out))
```

```python
import functools

import jax
import jax.numpy as jnp
from jax import lax
from jax.experimental import pallas as pl
from jax.experimental.pallas import tpu as pltpu

F32 = jnp.float32
BF16 = jnp.bfloat16

D_MODEL = 1024
PAGE_SIZE = 128
RG_BLOCKS = 8
RG_BLOCK_W = D_MODEL // RG_BLOCKS
RG_CONV_W = 4
RG_C = 8.0
GLA_HEADS = 4
GLA_DK = 128
GLA_DV = 256
GLA_RANK = 16
GLA_TAU = 16.0
GLA_CHUNK = 64
GLA_SUB = 16
NSA_HEADS = 16
NSA_G = 4
NSA_P = 4
NSA_DH = 64
CMP_BLOCK = 32
CMP_STRIDE = 16
SEL_BLOCK = 64
TOP_N = 16
WINDOW = 512
D_FF = 2816
KV_W = 512
EPS = 1e-6
NEG = -1e30
TINY = 1e-30
FORCE_SCORE = 1e4

COL_TILE = 512
C_MGATE = 0
C_RGY = 3072
C_RGX = 4096
C_GQ = 5120
C_GK = 5632
C_GV = 6144
C_GR = 7168
C_NQ = 8192
C_NCMP = 9216
C_NSLC = 9728
C_NWIN = 10240
C_SMALL = 10752
ZC = 11264
SMALL_W = 128
GATE_OFF = GLA_RANK

VMEM_LIMIT = 56 * 1024 * 1024


def _cp(sem, vmem=VMEM_LIMIT):
    return pltpu.CompilerParams(dimension_semantics=sem, vmem_limit_bytes=vmem)


def _nt(a, b):
    return lax.dot_general(a, b, (((1,), (1,)), ((), ())), preferred_element_type=F32)


def _tn(a, b):
    return lax.dot_general(a, b, (((0,), (0,)), ((), ())), preferred_element_type=F32)


def _dot(a, b):
    return jnp.dot(a, b, preferred_element_type=F32)


def _softplus(x):
    return jnp.maximum(x, 0.0) + jnp.log1p(jnp.exp(-jnp.abs(x)))


def _log_sigmoid(x):
    return jnp.minimum(x, 0.0) - jnp.log1p(jnp.exp(-jnp.abs(x)))


def _gelu_tanh(x):
    c = 0.7978845608028654
    return x * (0.5 * (1.0 + jnp.tanh(c * (x + 0.044715 * (x * x * x)))))


def _silu(x):
    return x * jax.nn.sigmoid(x)


def _seg_mean_sq(x, seg_ones):
    x2 = x * x
    hi = x2.astype(BF16)
    lo = (x2 - hi.astype(F32)).astype(BF16)
    outs = []
    for c in range(x.shape[1] // 128):
        sl = slice(c * 128, (c + 1) * 128)
        outs.append(_dot(hi[:, sl], seg_ones) + _dot(lo[:, sl], seg_ones))
    y = outs[0] if len(outs) == 1 else jnp.concatenate(outs, axis=1)
    return y * (1.0 / NSA_DH)


def _inproj_kernel(x_ref, g_ref, w_ref, o_ref, h_scr):
    @pl.when(pl.program_id(1) == 0)
    def _():
        x = x_ref[...]
        ms = jnp.mean(x * x, axis=-1, keepdims=True)
        h_scr[...] = (x * lax.rsqrt(ms + EPS) * g_ref[...]).astype(BF16)

    o_ref[...] = _dot(h_scr[...], w_ref[...])


def _inproj(x2d, g, w_perm):
    n = x2d.shape[0]
    tm = min(n, 1024)
    return pl.pallas_call(
        _inproj_kernel,
        grid=(n // tm, ZC // COL_TILE),
        in_specs=[
            pl.BlockSpec((tm, D_MODEL), lambda i, j: (i, 0)),
            pl.BlockSpec((1, D_MODEL), lambda i, j: (0, 0)),
            pl.BlockSpec((D_MODEL, COL_TILE), lambda i, j: (0, j)),
        ],
        out_specs=pl.BlockSpec((tm, COL_TILE), lambda i, j: (i, j)),
        out_shape=jax.ShapeDtypeStruct((n, ZC), F32),
        scratch_shapes=[pltpu.VMEM((tm, D_MODEL), BF16)],
        compiler_params=_cp(("arbitrary", "arbitrary")),
        name="inproj",
    )(x2d, g, w_perm)


def _rglru_gates(xc, wax_ref, ba, bx, lam):
    sp = _softplus(-lam)
    a_parts, u_parts = [], []
    for n in range(RG_BLOCKS):
        sl = slice(n * RG_BLOCK_W, (n + 1) * RG_BLOCK_W)
        xb = xc[:, sl]
        r = _dot(xb.astype(BF16), wax_ref[n])
        rec = jax.nn.sigmoid(r[:, :RG_BLOCK_W] + ba[:, sl])
        inp = jax.nn.sigmoid(r[:, RG_BLOCK_W:] + bx[:, sl])
        log_a = (-RG_C) * rec * sp[:, sl]
        a = jnp.exp(log_a)
        u = jnp.sqrt(-(jnp.tanh(log_a) * (a * a + 1.0))) * (inp * xb)
        a_parts.append(a)
        u_parts.append(u)
    return a_parts, u_parts


def _rglru_kernel(y_ref, x_ref, cw_ref, cb_ref, wax_ref, ba_ref, bx_ref, lam_ref,
                  o_ref, hl_ref, xp_scr, a_scr, u_scr, h_scr):
    t = pl.program_id(1)
    tt = x_ref.shape[1]

    @pl.when(t == 0)
    def _():
        xp_scr[0:8, :] = jnp.zeros((8, D_MODEL), F32)
        h_scr[...] = jnp.zeros_like(h_scr)

    x = x_ref[0]
    xp_scr[8:8 + tt, :] = x
    cw = cw_ref[...]
    xc = (cb_ref[...] + xp_scr[5:5 + tt, :] * cw[0:1] + xp_scr[6:6 + tt, :] * cw[1:2]
          + xp_scr[7:7 + tt, :] * cw[2:3] + x * cw[3:4])
    xp_scr[0:8, :] = xp_scr[tt:tt + 8, :]

    a_parts, u_parts = _rglru_gates(xc, wax_ref, ba_ref[...], bx_ref[...], lam_ref[...])
    for n in range(RG_BLOCKS):
        sl = slice(n * RG_BLOCK_W, (n + 1) * RG_BLOCK_W)
        a_scr[:, sl] = a_parts[n]
        u_scr[:, sl] = u_parts[n]

    row = lax.broadcasted_iota(jnp.int32, (8, D_MODEL), 0)

    def body(i, h):
        r0 = pl.multiple_of(i * 8, 8)
        a = a_scr[pl.ds(r0, 8), :]
        u = u_scr[pl.ds(r0, 8), :]
        for d in (1, 2, 4):
            a_sh = pltpu.roll(a, d, 0)
            u_sh = pltpu.roll(u, d, 0)
            m = row >= d
            u = jnp.where(m, a * u_sh + u, u)
            a = jnp.where(m, a * a_sh, a)
        hh = u + a * h
        u_scr[pl.ds(r0, 8), :] = hh
        return hh[7:8, :]

    h = lax.fori_loop(0, tt // 8, body, h_scr[0:1, :])
    h_scr[...] = jnp.broadcast_to(h, h_scr.shape)
    hl_ref[0] = jnp.broadcast_to(h, (8, D_MODEL))
    o_ref[0] = _gelu_tanh(y_ref[0]) * u_scr[...]


def _rglru_prompt(z3, cw, cb, wax, ba, bx, lam):
    b, t, _ = z3.shape
    tt = min(t, 512)
    row = lambda i, j: (0, 0)
    return pl.pallas_call(
        _rglru_kernel,
        grid=(b, t // tt),
        in_specs=[
            pl.BlockSpec((1, tt, D_MODEL), lambda i, j: (i, j, C_RGY // D_MODEL)),
            pl.BlockSpec((1, tt, D_MODEL), lambda i, j: (i, j, C_RGX // D_MODEL)),
            pl.BlockSpec((RG_CONV_W, D_MODEL), row),
            pl.BlockSpec((1, D_MODEL), row),
            pl.BlockSpec((RG_BLOCKS, RG_BLOCK_W, 2 * RG_BLOCK_W), lambda i, j: (0, 0, 0)),
            pl.BlockSpec((1, D_MODEL), row),
            pl.BlockSpec((1, D_MODEL), row),
            pl.BlockSpec((1, D_MODEL), row),
        ],
        out_specs=[
            pl.BlockSpec((1, tt, D_MODEL), lambda i, j: (i, j, 0)),
            pl.BlockSpec((1, 8, D_MODEL), lambda i, j: (i, 0, 0)),
        ],
        out_shape=[
            jax.ShapeDtypeStruct((b, t, D_MODEL), F32),
            jax.ShapeDtypeStruct((b, 8, D_MODEL), F32),
        ],
        scratch_shapes=[
            pltpu.VMEM((tt + 8, D_MODEL), F32),
            pltpu.VMEM((tt, D_MODEL), F32),
            pltpu.VMEM((tt, D_MODEL), F32),
            pltpu.VMEM((8, D_MODEL), F32),
        ],
        compiler_params=_cp(("arbitrary", "arbitrary")),
        name="rglru_prompt",
    )(z3, z3, cw, cb, wax, ba, bx, lam)


def _gla_chunk_head(q, k, v, lg, st):
    c = GLA_CHUNK
    r_i = lax.broadcasted_iota(jnp.int32, (c, c), 0)
    c_i = lax.broadcasted_iota(jnp.int32, (c, c), 1)
    tril = (r_i >= c_i).astype(F32)
    b = jnp.dot(tril, lg, preferred_element_type=F32, precision=lax.Precision.HIGHEST)
    kb = k.astype(BF16)
    vb = v.astype(BF16)
    o_state = _nt((q * jnp.exp(b)).astype(BF16), st.astype(BF16))
    sub = GLA_SUB
    t_row = lax.broadcasted_iota(jnp.int32, (sub, GLA_DK), 0)
    lane = lax.broadcasted_iota(jnp.int32, (sub, GLA_DK), 1)
    o_parts = []
    for blk in range(c // sub):
        r0 = blk * sub
        q_i = q[r0:r0 + sub]
        k_i = k[r0:r0 + sub]
        b_i = b[r0:r0 + sub]
        a_d = jnp.zeros((sub, GLA_DK), F32)
        for s in range(sub):
            diff = b_i - b_i[s:s + 1]
            dec = jnp.exp(jnp.where(t_row >= s, diff, NEG))
            col = jnp.sum(q_i * k_i[s:s + 1] * dec, axis=-1, keepdims=True)
            a_d = jnp.where(lane == s, col, a_d)
        o_i = _dot(a_d[:, :sub].astype(BF16), vb[r0:r0 + sub])
        if blk > 0:
            b_ref = b[r0 - 1:r0]
            qd = (q_i * jnp.exp(b_i - b_ref)).astype(BF16)
            kd = (k[:r0] * jnp.exp(b_ref - b[:r0])).astype(BF16)
            a_off = _nt(qd, kd)
            o_i = o_i + _dot(a_off.astype(BF16), vb[:r0])
        o_parts.append(o_i)
    o = o_state + jnp.concatenate(o_parts, axis=0)
    b_last = b[c - 1:c]
    kd = (k * jnp.exp(b_last - b)).astype(BF16)
    st_new = st * jnp.exp(b_last) + _tn(vb, kd)
    del kb
    return o, st_new


def _gla_kernel(q_ref, k_ref, v_ref, r_ref, ga_ref, wa_ref, ba_ref, gn_ref,
                o_ref, st_ref, st_scr):
    t = pl.program_id(1)
    tt = q_ref.shape[1]

    @pl.when(t == 0)
    def _():
        st_scr[...] = jnp.zeros_like(st_scr)

    gn = gn_ref[...]

    def body(ci, carry):
        r0 = pl.multiple_of(ci * GLA_CHUNK, GLA_CHUNK)
        rows = pl.ds(r0, GLA_CHUNK)
        xg = _dot(ga_ref[0, rows, :].astype(BF16), wa_ref[...]) + ba_ref[...]
        lg_all = _log_sigmoid(xg) * (1.0 / GLA_TAU)
        for h in range(GLA_HEADS):
            ksl = slice(h * GLA_DK, (h + 1) * GLA_DK)
            vsl = slice(h * GLA_DV, (h + 1) * GLA_DV)
            q = q_ref[0, rows, ksl] * (GLA_DK ** -0.5)
            k = k_ref[0, rows, ksl]
            v = v_ref[0, rows, vsl]
            o, st_new = _gla_chunk_head(q, k, v, lg_all[:, ksl], st_scr[h])
            st_scr[h] = st_new
            ms = jnp.mean(o * o, axis=-1, keepdims=True)
            on = o * lax.rsqrt(ms + EPS) * gn
            o_ref[0, rows, vsl] = on * _silu(r_ref[0, rows, vsl])
        return carry

    lax.fori_loop(0, tt // GLA_CHUNK, body, 0)
    st_ref[0] = st_scr[...]


def _gla_prompt(z3, wa_pad, ba, gn):
    b, t, _ = z3.shape
    tt = min(t, 256)
    hk = GLA_HEADS * GLA_DK
    hv = GLA_HEADS * GLA_DV
    row = lambda i, j: (0, 0)
    return pl.pallas_call(
        _gla_kernel,
        grid=(b, t // tt),
        in_specs=[
            pl.BlockSpec((1, tt, hk), lambda i, j: (i, j, C_GQ // hk)),
            pl.BlockSpec((1, tt, hk), lambda i, j: (i, j, C_GK // hk)),
            pl.BlockSpec((1, tt, hv), lambda i, j: (i, j, C_GV // hv)),
            pl.BlockSpec((1, tt, hv), lambda i, j: (i, j, C_GR // hv)),
            pl.BlockSpec((1, tt, SMALL_W), lambda i, j: (i, j, C_SMALL // SMALL_W)),
            pl.BlockSpec((SMALL_W, hk), row),
            pl.BlockSpec((1, hk), row),
            pl.BlockSpec((1, GLA_DV), row),
        ],
        out_specs=[
            pl.BlockSpec((1, tt, hv), lambda i, j: (i, j, 0)),
            pl.BlockSpec((1, GLA_HEADS, GLA_DV, GLA_DK), lambda i, j: (i, 0, 0, 0)),
        ],
        out_shape=[
            jax.ShapeDtypeStruct((b, t, hv), F32),
            jax.ShapeDtypeStruct((b, GLA_HEADS, GLA_DV, GLA_DK), F32),
        ],
        scratch_shapes=[pltpu.VMEM((GLA_HEADS, GLA_DV, GLA_DK), F32)],
        compiler_params=_cp(("arbitrary", "arbitrary")),
        name="gla_prompt",
    )(z3, z3, z3, z3, z3, wa_pad, ba, gn)


def _nsa_prep_kernel(nq_ref, slc_ref, win_ref, sm_ref, qg_ref, kg_ref, seg_ref,
                     q_ref, slc_o, ks_ref, vs_ref, win_o, kw_ref, vw_ref, gt_ref):
    seg = seg_ref[...]
    half = NSA_G * NSA_DH
    x = nq_ref[0]
    qn = x * lax.rsqrt(_seg_mean_sq(x, seg) + EPS) * qg_ref[...] * (NSA_DH ** -0.5)
    for h in range(NSA_HEADS):
        q_ref[0, h] = qn[:, h * NSA_DH:(h + 1) * NSA_DH].astype(BF16)
    for idx, (src, dst, kh, vh) in enumerate(((slc_ref, slc_o, ks_ref, vs_ref),
                                              (win_ref, win_o, kw_ref, vw_ref))):
        kv = src[0]
        kk = kv[:, :half]
        vv = kv[:, half:]
        kn = kk * lax.rsqrt(_seg_mean_sq(kk, seg) + EPS) * kg_ref[idx + 1:idx + 2, :]
        dst[0, :, :half] = kn
        dst[0, :, half:] = vv
        for g in range(NSA_G):
            sl = slice(g * NSA_DH, (g + 1) * NSA_DH)
            kh[0, g] = kn[:, sl].astype(BF16)
            vh[0, g] = vv[:, sl].astype(BF16)
    sig = jax.nn.sigmoid(sm_ref[0])
    for g in range(NSA_G):
        lo = GATE_OFF + g * NSA_P * 3
        gt_ref[0, g] = sig[:, lo:lo + NSA_P * 3]


def _nsa_prep(z3, qg_t, kg_t, seg_ones):
    b, t, _ = z3.shape
    tt = min(t, 512)
    row = lambda i, j: (0, 0)
    hm_q = pl.BlockSpec((1, NSA_HEADS, tt, NSA_DH), lambda i, j: (i, 0, j, 0))
    hm_kv = pl.BlockSpec((1, NSA_G, tt, NSA_DH), lambda i, j: (i, 0, j, 0))
    flat = pl.BlockSpec((1, tt, KV_W), lambda i, j: (i, j, 0))
    sd = jax.ShapeDtypeStruct
    return pl.pallas_call(
        _nsa_prep_kernel,
        grid=(b, t // tt),
        in_specs=[
            pl.BlockSpec((1, tt, D_MODEL), lambda i, j: (i, j, C_NQ // D_MODEL)),
            pl.BlockSpec((1, tt, KV_W), lambda i, j: (i, j, C_NSLC // KV_W)),
            pl.BlockSpec((1, tt, KV_W), lambda i, j: (i, j, C_NWIN // KV_W)),
            pl.BlockSpec((1, tt, SMALL_W), lambda i, j: (i, j, C_SMALL // SMALL_W)),
            pl.BlockSpec((1, D_MODEL), row),
            pl.BlockSpec((3, NSA_G * NSA_DH), row),
            pl.BlockSpec((128, 128), row),
        ],
        out_specs=[hm_q, flat, hm_kv, hm_kv, flat, hm_kv, hm_kv,
                   pl.BlockSpec((1, NSA_G, tt, NSA_P * 3), lambda i, j: (i, 0, j, 0))],
        out_shape=[
            sd((b, NSA_HEADS, t, NSA_DH), BF16),
            sd((b, t, KV_W), F32),
            sd((b, NSA_G, t, NSA_DH), BF16),
            sd((b, NSA_G, t, NSA_DH), BF16),
            sd((b, t, KV_W), F32),
            sd((b, NSA_G, t, NSA_DH), BF16),
            sd((b, NSA_G, t, NSA_DH), BF16),
            sd((b, NSA_G, t, NSA_P * 3), F32),
        ],
        compiler_params=_cp(("arbitrary", "arbitrary")),
        name="nsa_prep",
    )(z3, z3, z3, z3, qg_t, kg_t, seg_ones)


def _compress_half(load_rows, nch, ph_ref, pe_ref, sh_scr):
    half = NSA_G * NSA_DH
    lo = jnp.zeros((nch, half), F32)
    hi = jnp.zeros((nch, half), F32)
    for s in range(CMP_STRIDE):
        x = load_rows(s)
        lo += _dot((x + pe_ref[s:s + 1, :]).astype(BF16), ph_ref[s])
        hi += _dot((x + pe_ref[CMP_STRIDE + s:CMP_STRIDE + s + 1, :]).astype(BF16),
                   ph_ref[CMP_STRIDE + s])
    sh_scr[0:nch, :] = hi
    sh_scr[nch:nch + 8, :] = jnp.zeros((8, half), F32)
    return lo + sh_scr[1:nch + 1, :]


def _compress_kernel(x_ref, phk_ref, phv_ref, pek_ref, pev_ref, kg_ref, seg_ref,
                     kc_ref, vc_ref, sh_scr):
    nch = x_ref.shape[1]
    half = NSA_G * NSA_DH
    k = _compress_half(lambda s: x_ref[0, :, s * KV_W:s * KV_W + half], nch,
                       phk_ref, pek_ref, sh_scr)
    v = _compress_half(lambda s: x_ref[0, :, s * KV_W + half:(s + 1) * KV_W], nch,
                       phv_ref, pev_ref, sh_scr)
    kn = k * lax.rsqrt(_seg_mean_sq(k, seg_ref[...]) + EPS) * kg_ref[0:1, :]
    for g in range(NSA_G):
        sl = slice(g * NSA_DH, (g + 1) * NSA_DH)
        kc_ref[0, g] = kn[:, sl].astype(BF16)
        vc_ref[0, g] = v[:, sl].astype(BF16)


def _compress_prompt(cmp_rows, phk, phv, pek, pev, kg_t, seg_ones):
    b, nch, _ = cmp_rows.shape
    half = NSA_G * NSA_DH
    c3 = lambda i: (0, 0, 0)
    c2 = lambda i: (0, 0)
    hm = pl.BlockSpec((1, NSA_G, nch, NSA_DH), lambda i: (i, 0, 0, 0))
    return pl.pallas_call(
        _compress_kernel,
        grid=(b,),
        in_specs=[
            pl.BlockSpec((1, nch, CMP_STRIDE * KV_W), lambda i: (i, 0, 0)),
            pl.BlockSpec((CMP_BLOCK, half, half), c3),
            pl.BlockSpec((CMP_BLOCK, half, half), c3),
            pl.BlockSpec((CMP_BLOCK, half), c2),
            pl.BlockSpec((CMP_BLOCK, half), c2),
            pl.BlockSpec((3, half), c2),
            pl.BlockSpec((128, 128), c2),
        ],
        out_specs=[hm, hm],
        out_shape=[jax.ShapeDtypeStruct((b, NSA_G, nch, NSA_DH), BF16)] * 2,
        scratch_shapes=[pltpu.VMEM((nch + 8, half), F32)],
        compiler_params=_cp(("arbitrary",)),
        name="compress_prompt",
    )(cmp_rows, phk, phv, pek, pev, kg_t, seg_ones)


def _gated_heads(o, gates, c):
    qb = o.shape[0] // NSA_P
    parts = []
    for p in range(NSA_P):
        gcol = gates[:, p * 3 + c:p * 3 + c + 1]
        parts.append(o[p * qb:(p + 1) * qb] * gcol)
    return jnp.concatenate(parts, axis=1)


def _rank_select(score_t, n_keep):
    nj = score_t.shape[0]
    jrow = lax.broadcasted_iota(jnp.int32, score_t.shape, 0)
    rank = jnp.zeros(score_t.shape, F32)
    for jp in range(nj):
        r = score_t[jp:jp + 1, :]
        rank = rank + jnp.where(jrow > jp, jnp.where(r >= score_t, 1.0, 0.0),
                                jnp.where(r > score_t, 1.0, 0.0))
    return jnp.where(rank < n_keep, 1.0, 0.0)


def _cmp_attn_kernel(q_ref, kc_ref, vc_ref, gt_ref, sl_ref, m2s_ref,
                     o_ref, sel_ref, act_ref, *, n_keep):
    i = pl.program_id(2)
    qb = q_ref.shape[2]
    rows = NSA_P * qb
    nc = kc_ref.shape[2]
    q = q_ref[0].reshape(rows, NSA_DH)
    s = _nt(q, kc_ref[0, 0])
    r_i = lax.broadcasted_iota(jnp.int32, (rows, nc), 0)
    q_pos = i * qb + r_i % qb
    c_end = lax.broadcasted_iota(jnp.int32, (rows, nc), 1) * CMP_STRIDE + (CMP_BLOCK - 1)
    mask = c_end <= q_pos
    dist = (q_pos - c_end).astype(F32)
    sc = jnp.where(mask, s - sl_ref[0] * dist, NEG)
    m = jnp.max(sc, axis=-1, keepdims=True)
    e = jnp.where(mask, jnp.exp(sc - m), 0.0)
    p = e / jnp.maximum(jnp.sum(e, axis=-1, keepdims=True), TINY)
    o = _dot(p.astype(BF16), vc_ref[0, 0])
    o_ref[0] = _gated_heads(o, gt_ref[0, 0], 0)
    ps = p[0:qb]
    for pp in range(1, NSA_P):
        ps = ps + p[pp * qb:(pp + 1) * qb]
    hi = ps.astype(BF16)
    lo = (ps - hi.astype(F32)).astype(BF16)
    imp_t = _nt(m2s_ref[...], hi) + _nt(m2s_ref[...], lo)
    nj = imp_t.shape[0]
    jrow = lax.broadcasted_iota(jnp.int32, (nj, qb), 0)
    valid = jrow <= i
    forced = (jrow == 0) | (jrow == i) | (jrow == i - 1)
    score_t = jnp.where(valid, jnp.where(forced, FORCE_SCORE, imp_t), NEG)
    sel_t = _rank_select(score_t, n_keep)
    eye = (lax.broadcasted_iota(jnp.int32, (qb, qb), 0)
           == lax.broadcasted_iota(jnp.int32, (qb, qb), 1)).astype(BF16)
    sel_ref[0, 0] = _nt(eye, sel_t.astype(BF16))
    n_kt = act_ref.shape[3]
    in_tile = (lax.broadcasted_iota(jnp.int32, (n_kt, nj), 1) // (KEY_TILE // SEL_BLOCK)
               == lax.broadcasted_iota(jnp.int32, (n_kt, nj), 0)).astype(BF16)
    picks = jnp.max(_dot(in_tile, sel_t.astype(BF16)), axis=-1, keepdims=True)
    act_ref[0, 0, 0] = jnp.broadcast_to(jnp.where(picks > 0.5, 1, 0), (n_kt, 128))


def _cmp_attn_prompt(q_hm, kc, vc, gates, slope_col, m2s_t):
    b, _, t, _ = q_hm.shape
    qb = SEL_BLOCK
    nqb = t // qb
    nc = kc.shape[2]
    nsel = m2s_t.shape[0]
    n_kt = t // KEY_TILE
    return pl.pallas_call(
        functools.partial(_cmp_attn_kernel, n_keep=min(TOP_N, nsel)),
        grid=(b, NSA_G, nqb),
        in_specs=[
            pl.BlockSpec((1, NSA_P, qb, NSA_DH), lambda bi, g, i: (bi, g, i, 0)),
            pl.BlockSpec((1, 1, nc, NSA_DH), lambda bi, g, i: (bi, g, 0, 0)),
            pl.BlockSpec((1, 1, nc, NSA_DH), lambda bi, g, i: (bi, g, 0, 0)),
            pl.BlockSpec((1, 1, qb, NSA_P * 3), lambda bi, g, i: (bi, g, i, 0)),
            pl.BlockSpec((1, NSA_P * qb, 1), lambda bi, g, i: (g, 0, 0)),
            pl.BlockSpec((nsel, nc), lambda bi, g, i: (0, 0)),
        ],
        out_specs=[
            pl.BlockSpec((1, qb, NSA_P * NSA_DH), lambda bi, g, i: (bi, i, g)),
            pl.BlockSpec((1, 1, qb, nsel), lambda bi, g, i: (bi, g, i, 0)),
            pl.BlockSpec((1, 1, 1, n_kt, 128), lambda bi, g, i: (bi, g, i, 0, 0)),
        ],
        out_shape=[
            jax.ShapeDtypeStruct((b, t, D_MODEL), F32),
            jax.ShapeDtypeStruct((b, NSA_G, t, nsel), F32),
            jax.ShapeDtypeStruct((b, NSA_G, nqb, n_kt, 128), jnp.int32),
        ],
        compiler_params=_cp(("arbitrary", "arbitrary", "arbitrary")),
        name="cmp_attn_prompt",
    )(q_hm, kc, vc, gates, slope_col, m2s_t)


KEY_TILE = 512
WIN_SPAN = WINDOW + 128


def _slc_attn_kernel(act_ref, q_ref, k_ref, v_ref, gt_ref, sl_ref, sel_ref, ex_ref, o_ref,
                     sc_scr):
    i = pl.program_id(2)
    qb = q_ref.shape[2]
    step = (pl.program_id(0) * pl.num_programs(1) + pl.program_id(1)) * pl.num_programs(2) + i
    act_base = step * sc_scr.shape[0]
    sel_bias = ((sel_ref[0, 0] - 1.0) * (-NEG)).astype(BF16)
    q_abs = i * qb + lax.broadcasted_iota(jnp.int32, (qb, KEY_TILE), 0)
    lane = lax.broadcasted_iota(jnp.int32, (qb, KEY_TILE), 1)
    lane_row = lax.broadcasted_iota(jnp.int32, (1, KEY_TILE), 1)
    q_all = q_ref[0].reshape(NSA_P * qb, NSA_DH)
    slopes = [sl_ref[0, p * qb:p * qb + 1, :] for p in range(NSA_P)]
    n_tiles = (i * qb) // KEY_TILE + 1

    def tile_scores(kt, ms):
        k0 = pl.multiple_of(kt * KEY_TILE, KEY_TILE)
        kk = k_ref[0, 0, pl.ds(k0, KEY_TILE), :]
        mb = _dot(sel_bias, ex_ref[kt]) + jnp.where(k0 + lane > q_abs, NEG, 0.0)
        rel = (k0 - i * qb + lane_row).astype(F32)
        s_all = _nt(q_all, kk)
        out = []
        for p in range(NSA_P):
            sc = s_all[p * qb:(p + 1) * qb] + (mb + slopes[p] * rel)
            sc_scr[kt, p] = sc
            out.append(jnp.maximum(ms[p], jnp.max(sc, axis=-1, keepdims=True)))
        return tuple(out)

    def scores(kt, ms):
        return lax.cond(act_ref[act_base + kt] != 0, functools.partial(tile_scores, kt),
                        lambda ms: ms, ms)

    ms = lax.fori_loop(0, n_tiles, scores,
                       tuple(jnp.full((qb, 1), NEG, F32) for _ in range(NSA_P)))

    def tile_weigh(kt, carry):
        k0 = pl.multiple_of(kt * KEY_TILE, KEY_TILE)
        vv = v_ref[0, 0, pl.ds(k0, KEY_TILE), :]
        lsums, acc = carry
        out, es = [], []
        for p in range(NSA_P):
            e = jnp.exp(sc_scr[kt, p] - ms[p])
            lsum = lsums[p]
            for c in range(KEY_TILE // 128):
                lsum = lsum + e[:, c * 128:(c + 1) * 128]
            out.append(lsum)
            es.append(e.astype(BF16))
        return tuple(out), acc + _dot(jnp.concatenate(es, axis=0), vv)

    def weigh(kt, carry):
        return lax.cond(act_ref[act_base + kt] != 0, functools.partial(tile_weigh, kt),
                        lambda c: c, carry)

    init = (tuple(jnp.zeros((qb, 128), F32) for _ in range(NSA_P)),
            jnp.zeros((NSA_P * qb, NSA_DH), F32))
    lsums, acc = lax.fori_loop(0, n_tiles, weigh, init)
    l = jnp.concatenate([jnp.sum(ls, axis=-1, keepdims=True) for ls in lsums], axis=0)
    o_ref[0] = _gated_heads(acc / jnp.maximum(l, TINY), gt_ref[0, 0], 1)


def _win_attn_kernel(q_ref, k_ref, v_ref, gt_ref, sl_ref, o_ref):
    i = pl.program_id(2)
    qb = q_ref.shape[2]
    k0 = pl.multiple_of((jnp.maximum(i * qb - WINDOW, 0) // 128) * 128, 128)
    kk = k_ref[0, 0, pl.ds(k0, WIN_SPAN), :]
    vv = v_ref[0, 0, pl.ds(k0, WIN_SPAN), :]
    q_abs = i * qb + lax.broadcasted_iota(jnp.int32, (qb, WIN_SPAN), 0)
    dist = q_abs - (k0 + lax.broadcasted_iota(jnp.int32, (qb, WIN_SPAN), 1))
    mb = jnp.where((dist >= 0) & (dist <= WINDOW), 0.0, NEG)
    rel = (k0 - i * qb + lax.broadcasted_iota(jnp.int32, (1, WIN_SPAN), 1)).astype(F32)
    s_all = _nt(q_ref[0].reshape(NSA_P * qb, NSA_DH), kk)
    es, ls = [], []
    for p in range(NSA_P):
        sc = s_all[p * qb:(p + 1) * qb] + (mb + sl_ref[0, p * qb:p * qb + 1, :] * rel)
        m = jnp.max(sc, axis=-1, keepdims=True)
        e = jnp.exp(sc - m)
        ls.append(jnp.sum(e, axis=-1, keepdims=True))
        es.append(e.astype(BF16))
    o = _dot(jnp.concatenate(es, axis=0), vv) / jnp.maximum(jnp.concatenate(ls, axis=0), TINY)
    o_ref[0] = _gated_heads(o, gt_ref[0, 0], 2)


def _branch_attn_prompt(q_hm, k_hm, v_hm, gates, slope_col, sel=None, expand=None, active=None):
    b, _, t, _ = q_hm.shape
    qb = SEL_BLOCK
    nqb = t // qb
    in_specs = [
        pl.BlockSpec((1, NSA_P, qb, NSA_DH), lambda bi, g, i, *_: (bi, g, i, 0)),
        pl.BlockSpec((1, 1, t, NSA_DH), lambda bi, g, i, *_: (bi, g, 0, 0)),
        pl.BlockSpec((1, 1, t, NSA_DH), lambda bi, g, i, *_: (bi, g, 0, 0)),
        pl.BlockSpec((1, 1, qb, NSA_P * 3), lambda bi, g, i, *_: (bi, g, i, 0)),
        pl.BlockSpec((1, NSA_P * qb, 1), lambda bi, g, i, *_: (g, 0, 0)),
    ]
    args = [q_hm, k_hm, v_hm, gates, slope_col]
    scratch, prefetch = [], []
    if sel is None:
        body, name = _win_attn_kernel, "attn_win"
    else:
        scratch = [pltpu.VMEM((t // KEY_TILE, NSA_P, qb, KEY_TILE), F32)]
        nsel = sel.shape[-1]
        in_specs += [
            pl.BlockSpec((1, 1, qb, nsel), lambda bi, g, i, *_: (bi, g, i, 0)),
            pl.BlockSpec((expand.shape[0], nsel, KEY_TILE), lambda bi, g, i, *_: (0, 0, 0)),
        ]
        args += [sel, expand]
        prefetch = [active]
        body, name = _slc_attn_kernel, "attn_slc"
    grid_spec = pltpu.PrefetchScalarGridSpec(
        num_scalar_prefetch=len(prefetch),
        grid=(b, NSA_G, nqb),
        in_specs=in_specs,
        out_specs=pl.BlockSpec((1, qb, NSA_P * NSA_DH), lambda bi, g, i, *_: (bi, i, g)),
        scratch_shapes=scratch,
    )
    return pl.pallas_call(
        body,
        grid_spec=grid_spec,
        out_shape=jax.ShapeDtypeStruct((b, t, D_MODEL), F32),
        compiler_params=_cp(("arbitrary", "arbitrary", "arbitrary")),
        name=name,
    )(*prefetch, *args)


def _merge_kernel(x_ref, oa_ref, ob_ref, c1_ref, c2_ref, c3_ref, ma_ref, mb_ref, mc_ref,
                  wa_ref, wb_ref, wc_ref, wo_ref, o_ref):
    oc = c1_ref[...] + c2_ref[...] + c3_ref[...]
    merged = (jax.nn.sigmoid(ma_ref[...]) * _dot(oa_ref[...].astype(BF16), wa_ref[...])
              + jax.nn.sigmoid(mb_ref[...]) * _dot(ob_ref[...].astype(BF16), wb_ref[...])
              + jax.nn.sigmoid(mc_ref[...]) * _dot(oc.astype(BF16), wc_ref[...]))
    o_ref[...] = x_ref[...] + _dot(merged.astype(BF16), wo_ref[...])


def _merge(x2d, z2d, oa, ob, c1, c2, c3, wa, wb, wc, wo):
    n = x2d.shape[0]
    tt = min(n, 256)
    tok = pl.BlockSpec((tt, D_MODEL), lambda i: (i, 0))
    wsp = pl.BlockSpec((D_MODEL, D_MODEL), lambda i: (0, 0))
    zsp = [pl.BlockSpec((tt, D_MODEL), functools.partial(lambda i, c: (i, c), c=C_MGATE // D_MODEL + c))
           for c in range(3)]
    return pl.pallas_call(
        _merge_kernel,
        grid=(n // tt,),
        in_specs=[tok] * 6 + zsp + [wsp] * 4,
        out_specs=tok,
        out_shape=jax.ShapeDtypeStruct((n, D_MODEL), F32),
        compiler_params=_cp(("arbitrary",)),
        name="merge",
    )(x2d, oa, ob, c1, c2, c3, z2d, z2d, z2d, wa, wb, wc, wo)


FF_TILE = 1408


def _ffn_kernel(x_ref, g_ref, wg_ref, wu_ref, wd_ref, o_ref, h_scr, acc_scr):
    j = pl.program_id(1)

    @pl.when(j == 0)
    def _():
        x = x_ref[...]
        ms = jnp.mean(x * x, axis=-1, keepdims=True)
        h_scr[...] = (x * lax.rsqrt(ms + EPS) * g_ref[...]).astype(BF16)
        acc_scr[...] = x

    h = h_scr[...]
    act = _silu(_dot(h, wg_ref[...])) * _dot(h, wu_ref[...])
    acc_scr[...] += _dot(act.astype(BF16), wd_ref[...])

    @pl.when(j == pl.num_programs(1) - 1)
    def _():
        o_ref[...] = acc_scr[...]


def _ffn(x2d, g, wg, wu, wd):
    n = x2d.shape[0]
    tt = min(n, 512)
    return pl.pallas_call(
        _ffn_kernel,
        grid=(n // tt, D_FF // FF_TILE),
        in_specs=[
            pl.BlockSpec((tt, D_MODEL), lambda i, j: (i, 0)),
            pl.BlockSpec((1, D_MODEL), lambda i, j: (0, 0)),
            pl.BlockSpec((D_MODEL, FF_TILE), lambda i, j: (0, j)),
            pl.BlockSpec((D_MODEL, FF_TILE), lambda i, j: (0, j)),
            pl.BlockSpec((FF_TILE, D_MODEL), lambda i, j: (j, 0)),
        ],
        out_specs=pl.BlockSpec((tt, D_MODEL), lambda i, j: (i, 0)),
        out_shape=jax.ShapeDtypeStruct((n, D_MODEL), F32),
        scratch_shapes=[pltpu.VMEM((tt, D_MODEL), BF16), pltpu.VMEM((tt, D_MODEL), F32)],
        compiler_params=_cp(("arbitrary", "arbitrary")),
        name="ffn",
    )(x2d, g, wg, wu, wd)


IN_SIZES = (1024, 1024, 512, 512, 1024, 1024, 16, 1024, 512, 512, 512, 48, 3072)


def _perm_w_in(w):
    parts, start = [], 0
    for n in IN_SIZES:
        parts.append(w[:, start:start + n])
        start += n
    (rg_y, rg_x, g_q, g_k, g_v, g_r, g_a, n_q, n_cmp, n_slc, n_win, n_gate, m_gate) = parts
    pad = jnp.zeros((w.shape[0], ZC - C_SMALL - GLA_RANK - 3 * NSA_HEADS), w.dtype)
    return jnp.concatenate([m_gate, rg_y, rg_x, g_q, g_k, g_v, g_r, n_q, n_cmp, n_slc, n_win,
                            g_a, n_gate, pad], axis=1).astype(BF16)


def _block_diag4(phi):
    eye = jnp.eye(NSA_G, dtype=phi.dtype)
    out = jnp.einsum('gh,sde->sgdhe', eye, phi)
    return out.reshape(phi.shape[0], NSA_G * NSA_DH, NSA_G * NSA_DH)


def _alibi_slopes():
    h = jnp.arange(1, NSA_HEADS + 1, dtype=F32)
    return (2.0 ** (-8.0 * h / NSA_HEADS)).reshape(NSA_G, NSA_P)


def _layer_params(l, norm_mix_g, w_in, rg_conv_w, rg_conv_b, rg_w_a, rg_b_a, rg_w_x, rg_b_x,
                  rg_lambda, gla_w_a, gla_b_a, gla_norm_g, nsa_q_norm_g, nsa_k_norm_g, nsa_phi_k,
                  nsa_phi_v, nsa_cmp_pe, w_rg_out, w_gla_out, w_nsa_out, w_out, norm_ffn_g,
                  w_ffn_gate, w_ffn_up, w_ffn_down):
    half = NSA_G * NSA_DH
    p = {}
    p['norm_mix'] = norm_mix_g[l][None, :]
    p['w_in'] = _perm_w_in(w_in[l])
    p['cw'] = rg_conv_w[l]
    p['cb'] = rg_conv_b[l][None, :]
    p['wax'] = jnp.concatenate([rg_w_a[l], rg_w_x[l]], axis=-1).astype(BF16)
    p['ba'] = rg_b_a[l][None, :]
    p['bx'] = rg_b_x[l][None, :]
    p['lam'] = rg_lambda[l][None, :]
    p['gla_wa'] = jnp.pad(gla_w_a[l], ((0, SMALL_W - GLA_RANK), (0, 0))).astype(BF16)
    p['gla_ba'] = gla_b_a[l][None, :]
    p['gla_gn'] = gla_norm_g[l][None, :]
    p['qg'] = jnp.tile(nsa_q_norm_g[l], NSA_HEADS)[None, :]
    p['kg'] = jnp.tile(nsa_k_norm_g[l], (1, NSA_G))
    p['phk'] = _block_diag4(nsa_phi_k[l]).astype(BF16)
    p['phv'] = _block_diag4(nsa_phi_v[l]).astype(BF16)
    p['pek'] = jnp.tile(nsa_cmp_pe[l, 0], (1, NSA_G))
    p['pev'] = jnp.tile(nsa_cmp_pe[l, 1], (1, NSA_G))
    p['w_rg_out'] = w_rg_out[l].astype(BF16)
    p['w_gla_out'] = w_gla_out[l].astype(BF16)
    p['w_nsa_out'] = w_nsa_out[l].astype(BF16)
    p['w_out'] = w_out[l].astype(BF16)
    p['norm_ffn'] = norm_ffn_g[l][None, :]
    p['wg'] = w_ffn_gate[l].astype(BF16)
    p['wu'] = w_ffn_up[l].astype(BF16)
    p['wd'] = w_ffn_down[l].astype(BF16)
    del half
    return p


def _seg_ones():
    r = jnp.arange(128) // NSA_DH
    return (r[:, None] == r[None, :]).astype(BF16)


def _cmp_to_sel_t(n_cmp_rows, n_cmp_valid, n_sel):
    start = jnp.arange(n_cmp_rows)[None, :] * CMP_STRIDE
    sel = jnp.arange(n_sel)[:, None] * SEL_BLOCK
    ok = (start < sel + SEL_BLOCK) & (start + CMP_BLOCK > sel) & (jnp.arange(n_cmp_rows)[None, :] < n_cmp_valid)
    return ok.astype(BF16)


def _expand_blocks(n_sel, n_keys):
    key_blk = jnp.arange(n_keys) // SEL_BLOCK
    e = (jnp.arange(n_sel)[:, None] == key_blk[None, :]).astype(BF16)
    return e.reshape(n_sel, n_keys // KEY_TILE, KEY_TILE).transpose(1, 0, 2)


def _slope_cols(rows_per_head):
    return jnp.repeat(_alibi_slopes(), rows_per_head, axis=1)[:, :, None]


def _prompt_layer(x, p):
    b, t, _ = x.shape
    n = b * t
    z2 = _inproj(x.reshape(n, D_MODEL), p['norm_mix'], p['w_in'])
    z3 = z2.reshape(b, t, ZC)
    o_a, h_last = _rglru_prompt(z3, p['cw'], p['cb'], p['wax'], p['ba'], p['bx'], p['lam'])
    o_b, st = _gla_prompt(z3, p['gla_wa'], p['gla_ba'], p['gla_gn'])
    seg = _seg_ones()
    q_hm, slc_n, ks, vs, win_n, kw, vw, gates = _nsa_prep(z3, p['qg'], p['kg'], seg)
    nch = t // CMP_STRIDE
    cmp_flat = z3[:, :, C_NCMP:C_NCMP + KV_W]
    kc, vc = _compress_prompt(cmp_flat.reshape(b, nch, CMP_STRIDE * KV_W), p['phk'], p['phv'],
                              p['pek'], p['pev'], p['kg'], seg)
    n_sel = t // SEL_BLOCK
    m2s_t = _cmp_to_sel_t(nch, nch - 1, n_sel)
    slope_col = _slope_cols(SEL_BLOCK)
    o_cmp, sel, act = _cmp_attn_prompt(q_hm, kc, vc, gates, slope_col, m2s_t)
    expand = _expand_blocks(n_sel, t)
    o_slc = _branch_attn_prompt(q_hm, ks, vs, gates, slope_col, sel, expand,
                                act[..., 0].reshape(-1))
    o_win = _branch_attn_prompt(q_hm, kw, vw, gates, slope_col)
    x2 = _merge(x.reshape(n, D_MODEL), z2, o_a.reshape(n, D_MODEL), o_b.reshape(n, D_MODEL),
                o_cmp.reshape(n, D_MODEL), o_slc.reshape(n, D_MODEL), o_win.reshape(n, D_MODEL),
                p['w_rg_out'], p['w_gla_out'], p['w_nsa_out'], p['w_out'])
    y = _ffn(x2, p['norm_ffn'], p['wg'], p['wu'], p['wd']).reshape(b, t, D_MODEL)
    kv_shape = (b, t, 2, NSA_G, NSA_DH)
    cmp_kv = cmp_flat.reshape(kv_shape)
    slc_kv = slc_n.reshape(kv_shape)
    win_state = win_n[:, t - min(WINDOW, t):].reshape(b, min(WINDOW, t), 2, NSA_G, NSA_DH)
    s_new = jnp.swapaxes(st, -1, -2)
    conv_new = z3[:, t - (RG_CONV_W - 1):, C_RGX:C_RGX + D_MODEL]
    return y, (cmp_kv, slc_kv, win_state, s_new, h_last[:, 7], conv_new)


def _rglru_step_kernel(y_ref, x_ref, c0_ref, c1_ref, c2_ref, h0_ref, cw_ref, cb_ref, wax_ref,
                       ba_ref, bx_ref, lam_ref, o_ref, h_ref):
    cw = cw_ref[...]
    x = x_ref[...]
    xc = (cb_ref[...] + c0_ref[...] * cw[0:1] + c1_ref[...] * cw[1:2]
          + c2_ref[...] * cw[2:3] + x * cw[3:4])
    a_parts, u_parts = _rglru_gates(xc, wax_ref, ba_ref[...], bx_ref[...], lam_ref[...])
    a = jnp.concatenate(a_parts, axis=1)
    u = jnp.concatenate(u_parts, axis=1)
    h = u + a * h0_ref[...]
    h_ref[...] = h
    o_ref[...] = _gelu_tanh(y_ref[...]) * h


def _rglru_sample(z2, conv_state, h0, cw, cb, wax, ba, bx, lam):
    n = z2.shape[0]
    tok = lambda c: pl.BlockSpec((n, D_MODEL), functools.partial(lambda i, c: (0, c), c=c))
    full = pl.BlockSpec((n, D_MODEL), lambda i: (0, 0))
    row = pl.BlockSpec((1, D_MODEL), lambda i: (0, 0))
    return pl.pallas_call(
        _rglru_step_kernel,
        grid=(1,),
        in_specs=[tok(C_RGY // D_MODEL), tok(C_RGX // D_MODEL), full, full, full, full,
                  pl.BlockSpec((RG_CONV_W, D_MODEL), lambda i: (0, 0)), row,
                  pl.BlockSpec((RG_BLOCKS, RG_BLOCK_W, 2 * RG_BLOCK_W), lambda i: (0, 0, 0)),
                  row, row, row],
        out_specs=[full, full],
        out_shape=[jax.ShapeDtypeStruct((n, D_MODEL), F32)] * 2,
        compiler_params=_cp(("arbitrary",)),
        name="rglru_sample",
    )(z2, z2, conv_state[:, 0], conv_state[:, 1], conv_state[:, 2], h0, cw, cb, wax, ba, bx, lam)


def _row_to_col(row):
    n = row.shape[1]
    eye = lax.broadcasted_iota(jnp.int32, (n, n), 0) == lax.broadcasted_iota(jnp.int32, (n, n), 1)
    return jnp.sum(jnp.where(eye, jnp.broadcast_to(row, (n, n)), 0.0), axis=1, keepdims=True)


def _gla_step_kernel(q_ref, k_ref, v_ref, r_ref, ga_ref, s_ref, wa_ref, ba_ref, gn_ref,
                     o_ref, so_ref):
    ga = jnp.broadcast_to(ga_ref[0], (8, SMALL_W)).astype(BF16)
    xg = _dot(ga, wa_ref[...])[0:1] + ba_ref[...]
    lg = _log_sigmoid(xg) * (1.0 / GLA_TAU)
    for h in range(GLA_HEADS):
        ksl = slice(h * GLA_DK, (h + 1) * GLA_DK)
        vsl = slice(h * GLA_DV, (h + 1) * GLA_DV)
        q = q_ref[0, :, ksl] * (GLA_DK ** -0.5)
        k = k_ref[0, :, ksl]
        v = v_ref[0, :, vsl]
        b = lg[:, ksl]
        eb = jnp.exp(b)
        s_old = s_ref[0, h]
        a = jnp.sum(q * k, axis=-1, keepdims=True)
        qe = jnp.broadcast_to(q * eb, (8, GLA_DK)).astype(BF16)
        o = a * v + _dot(qe, s_old.astype(BF16))[0:1]
        so_ref[0, h] = _row_to_col(eb) * s_old + _row_to_col(k) * v
        ms = jnp.mean(o * o, axis=-1, keepdims=True)
        on = o * lax.rsqrt(ms + EPS) * gn_ref[...]
        o_ref[0, :, vsl] = on * _silu(r_ref[0, :, vsl])


def _gla_sample(z3, state, wa_pad, ba, gn):
    n = z3.shape[0]
    hk = GLA_HEADS * GLA_DK
    hv = GLA_HEADS * GLA_DV
    row = lambda i: (0, 0)
    st = pl.BlockSpec((1, GLA_HEADS, GLA_DK, GLA_DV), lambda i: (i, 0, 0, 0))
    return pl.pallas_call(
        _gla_step_kernel,
        grid=(n,),
        in_specs=[
            pl.BlockSpec((1, 1, hk), lambda i: (i, 0, C_GQ // hk)),
            pl.BlockSpec((1, 1, hk), lambda i: (i, 0, C_GK // hk)),
            pl.BlockSpec((1, 1, hv), lambda i: (i, 0, C_GV // hv)),
            pl.BlockSpec((1, 1, hv), lambda i: (i, 0, C_GR // hv)),
            pl.BlockSpec((1, 1, SMALL_W), lambda i: (i, 0, C_SMALL // SMALL_W)),
            st,
            pl.BlockSpec((SMALL_W, hk), row),
            pl.BlockSpec((1, hk), row),
            pl.BlockSpec((1, GLA_DV), row),
        ],
        out_specs=[pl.BlockSpec((1, 1, hv), lambda i: (i, 0, 0)), st],
        out_shape=[jax.ShapeDtypeStruct((n, 1, hv), F32),
                   jax.ShapeDtypeStruct(state.shape, F32)],
        compiler_params=_cp(("arbitrary",)),
        name="gla_sample",
    )(z3, z3, z3, z3, z3, state, wa_pad, ba, gn)


PAGES_PER_ITER = 8


def _group_fold(o):
    grp = lax.broadcasted_iota(jnp.int32, (NSA_HEADS, NSA_DH), 0) // NSA_P
    out = jnp.zeros((NSA_HEADS, NSA_DH), F32)
    for g in range(NSA_G):
        out = out + jnp.where(grp == g, o[:, g * NSA_DH:(g + 1) * NSA_DH], 0.0)
    return out


def _nsa_sample_kernel(pt_ref, q_ref, slc_new_ref, win_new_ref, sm_ref, wb_ref, sl_ref,
                       phk_ref, phv_ref, pek_ref, pev_ref, kg_ref, seg_ref, m2s_ref,
                       cmp_pool, slc_pool, o_ref, wo_ref, buf, xs_scr, sh_scr, sc_scr, sem,
                       *, layer_off, n_keep):
    bi = pl.program_id(0)
    n_pages = pt_ref.shape[1]
    past = n_pages * PAGE_SIZE
    half = NSA_G * NSA_DH
    nch = past // CMP_STRIDE
    cpp = PAGE_SIZE // CMP_STRIDE

    def page_copy(pool, pg):
        return pltpu.make_async_copy(pool.at[layer_off + pt_ref[bi, pg]], buf.at[pg], sem)

    def gather_start(pool):
        def start(pg, c):
            page_copy(pool, pg).start()
            return c

        lax.fori_loop(0, n_pages, start, 0)

    def gather_wait(pool):
        def wait(pg, c):
            page_copy(pool, pg).wait()
            return c

        lax.fori_loop(0, n_pages, wait, 0)

    def transpose_pool_half(row0):
        def fill(it, c):
            for u in range(PAGES_PER_ITER):
                pg = it * PAGES_PER_ITER + u
                xt = buf[pg, row0:row0 + half, :].T
                r0 = pl.multiple_of(pg * PAGE_SIZE, PAGE_SIZE)
                xs_scr[0, pl.ds(r0, PAGE_SIZE), :] = xt[:, 0:128]
                xs_scr[1, pl.ds(r0, PAGE_SIZE), :] = xt[:, 128:256]
            return c

        lax.fori_loop(0, n_pages // PAGES_PER_ITER, fill, 0)

    def chunk_rows(s):
        rows = pl.ds(s, nch, stride=CMP_STRIDE)
        return jnp.concatenate([xs_scr[0, rows, :], xs_scr[1, rows, :]], axis=1)

    q16 = q_ref[0]
    head = lax.broadcasted_iota(jnp.int32, (NSA_HEADS, half), 0)
    lane = lax.broadcasted_iota(jnp.int32, (NSA_HEADS, half), 1)
    qblk = jnp.where(lane // NSA_DH == head // NSA_P,
                     jnp.concatenate([q16] * NSA_G, axis=1), jnp.zeros((), BF16))
    slope = sl_ref[...]
    sig = jax.nn.sigmoid(sm_ref[0])
    lane_s = lax.broadcasted_iota(jnp.int32, (NSA_HEADS, SMALL_W), 1)
    head_s = lax.broadcasted_iota(jnp.int32, (NSA_HEADS, SMALL_W), 0)

    def gate_col(c):
        return jnp.sum(jnp.where(lane_s == GATE_OFF + 3 * head_s + c, sig, 0.0),
                       axis=1, keepdims=True)

    gather_start(cmp_pool)
    gather_wait(cmp_pool)
    transpose_pool_half(0)
    kc = _compress_half(chunk_rows, nch, phk_ref, pek_ref, sh_scr)
    transpose_pool_half(half)
    gather_start(slc_pool)
    vc = _compress_half(chunk_rows, nch, phv_ref, pev_ref, sh_scr)
    kn = kc * lax.rsqrt(_seg_mean_sq(kc, seg_ref[...]) + EPS) * kg_ref[0:1, :]
    s = _nt(qblk, kn.astype(BF16))
    c_end = lax.broadcasted_iota(jnp.int32, (NSA_HEADS, nch), 1) * CMP_STRIDE + (CMP_BLOCK - 1)
    mask = c_end <= past
    sc = jnp.where(mask, s - slope * (past - c_end).astype(F32), NEG)
    m = jnp.max(sc, axis=-1, keepdims=True)
    e = jnp.where(mask, jnp.exp(sc - m), 0.0)
    p = e / jnp.maximum(jnp.sum(e, axis=-1, keepdims=True), TINY)
    o_cmp = _group_fold(_dot(p.astype(BF16), vc.astype(BF16)))
    ps_rows = []
    for g in range(NSA_G):
        r = p[g * NSA_P:g * NSA_P + 1]
        for pp in range(1, NSA_P):
            r = r + p[g * NSA_P + pp:g * NSA_P + pp + 1]
        ps_rows.append(r)
    ps = jnp.concatenate(ps_rows + [jnp.zeros((8 - NSA_G, nch), F32)], axis=0)
    hi = ps.astype(BF16)
    lo = (ps - hi.astype(F32)).astype(BF16)
    imp = _dot(hi, m2s_ref[...]) + _dot(lo, m2s_ref[...])
    nselp = imp.shape[1]
    cur = past // SEL_BLOCK
    jl = lax.broadcasted_iota(jnp.int32, (8, nselp), 1)
    forced = (jl == 0) | (jl == cur) | (jl == cur - 1)
    score = jnp.where(jl <= cur, jnp.where(forced, FORCE_SCORE, imp), NEG)
    jr = lax.broadcasted_iota(jnp.int32, (nselp, nselp), 0)
    jc = lax.broadcasted_iota(jnp.int32, (nselp, nselp), 1)
    sel_rows = []
    for g in range(NSA_G):
        row = score[g:g + 1]
        col = _row_to_col(row)
        ahead = jnp.where(jc > jr, jnp.where(col >= row, 1.0, 0.0), jnp.where(col > row, 1.0, 0.0))
        rank = jnp.sum(ahead, axis=0, keepdims=True)
        sel_g = jnp.where(rank < n_keep, 1.0, 0.0)
        sel_rows += [sel_g] * NSA_P
    sel16 = jnp.concatenate(sel_rows, axis=0).astype(BF16)

    qf = qblk.astype(F32)

    def self_score(row):
        return jnp.sum(qf * row[:, 0:half].astype(BF16).astype(F32), axis=-1, keepdims=True)

    def add_self(m, l, acc, s_new, row):
        v_new = row[:, half:2 * half].astype(BF16).astype(F32)
        m_new = jnp.maximum(m, s_new)
        alpha = jnp.exp(m - m_new)
        e_new = jnp.exp(s_new - m_new)
        l = alpha * l + e_new
        acc = alpha * acc + e_new.astype(BF16).astype(F32) * v_new
        return l, acc

    gather_wait(slc_pool)
    lane_p = lax.broadcasted_iota(jnp.int32, (NSA_HEADS, PAGE_SIZE), 1)
    j_i = lax.broadcasted_iota(jnp.int32, (nselp, PAGE_SIZE), 0)
    blk_in_page = lax.broadcasted_iota(jnp.int32, (nselp, PAGE_SIZE), 1) // SEL_BLOCK
    bpp = PAGE_SIZE // SEL_BLOCK

    def score(it, c):
        for u in range(PAGES_PER_ITER):
            pg = it * PAGES_PER_ITER + u
            s = _dot(qblk, buf[pg, 0:half, :].astype(BF16))
            in_page = (j_i == pg * bpp + blk_in_page).astype(BF16)
            picked = _dot(sel16, in_page) > 0.5
            dist = (past - (pg * PAGE_SIZE + lane_p)).astype(F32)
            sc_scr[pg] = jnp.where(picked, s - slope * dist, NEG)
        return c

    lax.fori_loop(0, n_pages // PAGES_PER_ITER, score, 0)
    s_self = self_score(slc_new_ref[0])
    sc_all = sc_scr[...]
    m = jnp.max(jnp.max(sc_all, axis=0), axis=-1, keepdims=True)
    m_all = jnp.maximum(m, s_self)
    e_all = jnp.exp(sc_all - m_all[None])
    l = jnp.sum(jnp.sum(e_all, axis=0), axis=-1, keepdims=True)
    sc_scr[...] = e_all

    def weigh(it, acc):
        parts = []
        for u in range(PAGES_PER_ITER):
            pg = it * PAGES_PER_ITER + u
            parts.append(_nt(sc_scr[pg].astype(BF16), buf[pg, half:2 * half, :].astype(BF16)))
        while len(parts) > 1:
            parts = [a + b for a, b in zip(parts[0::2], parts[1::2])]
        return acc + parts[0]

    acc = lax.fori_loop(0, n_pages // PAGES_PER_ITER, weigh, jnp.zeros((NSA_HEADS, half), F32))
    l, acc = add_self(m_all, l, acc, s_self, slc_new_ref[0])
    o_slc = _group_fold(acc / jnp.maximum(l, TINY))

    w_len = wb_ref.shape[1]
    kk = wb_ref[0, :, 0:half].astype(BF16)
    vv = wb_ref[0, :, half:2 * half].astype(BF16)
    s = _nt(qblk, kk)
    dist = w_len - lax.broadcasted_iota(jnp.int32, (NSA_HEADS, w_len), 1)
    mask = dist <= WINDOW
    sc = jnp.where(mask, s - slope * dist.astype(F32), NEG)
    m = jnp.max(sc, axis=-1, keepdims=True)
    e = jnp.where(mask, jnp.exp(sc - m), 0.0)
    l = jnp.sum(e, axis=-1, keepdims=True)
    acc = _dot(e.astype(BF16), vv)
    l, acc = add_self(m, l, acc, self_score(win_new_ref[0]), win_new_ref[0])
    o_win = _group_fold(acc / jnp.maximum(l, TINY))

    o_ref[0] = gate_col(0) * o_cmp + gate_col(1) * o_slc + gate_col(2) * o_win
    wo_ref[0, 0:w_len - 1, :] = wb_ref[0, 1:w_len, :]
    wo_ref[0, w_len - 1:w_len, :] = win_new_ref[0]


def _nsa_sample(page_table, q_bhd, slc_new, win_new, z3, win_buf, cmp_pool, slc_pool, layer_off,
                p, seg):
    n, n_pages = page_table.shape
    past = n_pages * PAGE_SIZE
    half = NSA_G * NSA_DH
    nch = past // CMP_STRIDE
    n_sel = past // SEL_BLOCK + 1
    nselp = -(-n_sel // 128) * 128
    w_len = win_buf.shape[1]
    m2s = jnp.pad(_cmp_to_sel_t(nch, nch - 1, n_sel).T, ((0, 0), (0, nselp - n_sel)))
    slope16 = _alibi_slopes().reshape(NSA_HEADS, 1)
    c2 = lambda i, pt: (0, 0)
    c3 = lambda i, pt: (0, 0, 0)
    once = pl.Buffered(1)
    grid_spec = pltpu.PrefetchScalarGridSpec(
        num_scalar_prefetch=1,
        grid=(n,),
        in_specs=[
            pl.BlockSpec((1, NSA_HEADS, NSA_DH), lambda i, pt: (i, 0, 0)),
            pl.BlockSpec((1, 1, KV_W), lambda i, pt: (i, 0, 0)),
            pl.BlockSpec((1, 1, KV_W), lambda i, pt: (i, 0, 0)),
            pl.BlockSpec((1, 1, SMALL_W), lambda i, pt: (i, 0, C_SMALL // SMALL_W)),
            pl.BlockSpec((1, w_len, KV_W), lambda i, pt: (i, 0, 0)),
            pl.BlockSpec((NSA_HEADS, 1), c2),
            pl.BlockSpec((CMP_BLOCK, half, half), c3, pipeline_mode=once),
            pl.BlockSpec((CMP_BLOCK, half, half), c3, pipeline_mode=once),
            pl.BlockSpec((CMP_BLOCK, half), c2),
            pl.BlockSpec((CMP_BLOCK, half), c2),
            pl.BlockSpec((3, half), c2),
            pl.BlockSpec((128, 128), c2),
            pl.BlockSpec((nch, nselp), c2),
            pl.BlockSpec(memory_space=pl.ANY),
            pl.BlockSpec(memory_space=pl.ANY),
        ],
        out_specs=[
            pl.BlockSpec((1, NSA_HEADS, NSA_DH), lambda i, pt: (i, 0, 0)),
            pl.BlockSpec((1, w_len, KV_W), lambda i, pt: (i, 0, 0)),
        ],
        scratch_shapes=[
            pltpu.VMEM((n_pages, KV_W, PAGE_SIZE), F32),
            pltpu.VMEM((2, past, 128), F32),
            pltpu.VMEM((nch + 8, half), F32),
            pltpu.VMEM((n_pages, NSA_HEADS, PAGE_SIZE), F32),
            pltpu.SemaphoreType.DMA(()),
        ],
    )
    return pl.pallas_call(
        functools.partial(_nsa_sample_kernel, layer_off=layer_off, n_keep=min(TOP_N, n_sel)),
        grid_spec=grid_spec,
        out_shape=[jax.ShapeDtypeStruct((n, NSA_HEADS, NSA_DH), F32),
                   jax.ShapeDtypeStruct((n, w_len, KV_W), F32)],
        compiler_params=_cp(("arbitrary",)),
        name="nsa_sample",
    )(page_table, q_bhd, slc_new, win_new, z3, win_buf, slope16, p['phk'], p['phv'], p['pek'],
      p['pev'], p['kg'], seg, m2s, cmp_pool, slc_pool)


def _sample_layer(x, p, l, cmp_pool, slc_pool, win_buf, state_gla, state_rglru, state_conv,
                  page_table, n_pool):
    n = x.shape[0]
    xs = x.reshape(n, D_MODEL)
    z2 = _inproj(xs, p['norm_mix'], p['w_in'])
    z3 = z2.reshape(n, 1, ZC)
    o_a, h_new = _rglru_sample(z2, state_conv, state_rglru, p['cw'], p['cb'], p['wax'], p['ba'],
                               p['bx'], p['lam'])
    o_b, s_new = _gla_sample(z3, state_gla, p['gla_wa'], p['gla_ba'], p['gla_gn'])
    seg = _seg_ones()
    q_hm, slc_n, _, _, win_n, _, _, _ = _nsa_prep(z2.reshape(1, n, ZC), p['qg'], p['kg'], seg)
    q_bhd = jnp.swapaxes(q_hm[0], 0, 1)
    slc_new = slc_n.reshape(n, 1, KV_W)
    win_new = win_n.reshape(n, 1, KV_W)
    w_len = win_buf.shape[1]
    o_c, win_state = _nsa_sample(page_table, q_bhd, slc_new, win_new, z3,
                                 win_buf.reshape(n, w_len, KV_W), cmp_pool, slc_pool, l * n_pool,
                                 p, seg)
    zero = jnp.zeros((n, D_MODEL), F32)
    x2 = _merge(xs, z2, o_a, o_b.reshape(n, D_MODEL), o_c.reshape(n, D_MODEL), zero, zero,
                p['w_rg_out'], p['w_gla_out'], p['w_nsa_out'], p['w_out'])
    y = _ffn(x2, p['norm_ffn'], p['wg'], p['wu'], p['wd']).reshape(n, 1, D_MODEL)
    kv_shape = (n, 1, 2, NSA_G, NSA_DH)
    cmp_kv = z2[:, C_NCMP:C_NCMP + KV_W].reshape(kv_shape)
    conv_new = jnp.concatenate([state_conv[:, 1:], z2[:, None, C_RGX:C_RGX + D_MODEL]], axis=1)
    return y, (cmp_kv, slc_n.reshape(kv_shape), win_state.reshape(n, w_len, 2, NSA_G, NSA_DH),
               s_new, h_new, conv_new)


def kernel(x_prompt, x_sample, cache_cmp_kv, cache_slc_kv, cache_win_kv, state_gla, state_rglru,
           state_conv, page_table, norm_mix_g, w_in, rg_conv_w, rg_conv_b, rg_w_a, rg_b_a, rg_w_x,
           rg_b_x, rg_lambda, gla_w_a, gla_b_a, gla_norm_g, nsa_q_norm_g, nsa_k_norm_g, nsa_phi_k,
           nsa_phi_v, nsa_cmp_pe, w_rg_out, w_gla_out, w_nsa_out, w_out, norm_ffn_g, w_ffn_gate,
           w_ffn_up, w_ffn_down):
    depth, n_pool = cache_cmp_kv.shape[:2]
    pool_shape = (depth * n_pool, KV_W, PAGE_SIZE)
    cmp_pool = jnp.transpose(cache_cmp_kv, (0, 1, 3, 4, 5, 2)).reshape(pool_shape)
    slc_pool = jnp.transpose(cache_slc_kv, (0, 1, 3, 4, 5, 2)).reshape(pool_shape)
    yp, ys = x_prompt, x_sample
    p_out = [[] for _ in range(6)]
    s_out = [[] for _ in range(6)]
    for l in range(depth):
        p = _layer_params(l, norm_mix_g, w_in, rg_conv_w, rg_conv_b, rg_w_a, rg_b_a, rg_w_x,
                          rg_b_x, rg_lambda, gla_w_a, gla_b_a, gla_norm_g, nsa_q_norm_g,
                          nsa_k_norm_g, nsa_phi_k, nsa_phi_v, nsa_cmp_pe, w_rg_out, w_gla_out,
                          w_nsa_out, w_out, norm_ffn_g, w_ffn_gate, w_ffn_up, w_ffn_down)
        yp, p_new = _prompt_layer(yp, p)
        ys, s_new = _sample_layer(ys, p, l, cmp_pool, slc_pool, cache_win_kv[l], state_gla[l],
                                  state_rglru[l], state_conv[l], page_table, n_pool)
        for i in range(6):
            p_out[i].append(p_new[i])
            s_out[i].append(s_new[i])
    return ((yp, ys) + tuple(jnp.stack(a) for a in p_out) + tuple(jnp.stack(a) for a in s_out))
```

```python
import functools

import jax
import jax.numpy as jnp
import numpy as np
from jax import lax
from jax.experimental import pallas as pl
from jax.experimental.pallas import tpu as pltpu

F32 = jnp.float32
BF16 = jnp.bfloat16

D_MODEL = 1024
PAGE_SIZE = 128
RG_BLOCKS = 8
RG_BLOCK_W = D_MODEL // RG_BLOCKS
RG_CONV_W = 4
RG_C = 8.0
GLA_HEADS = 4
GLA_DK = 128
GLA_DV = 256
GLA_RANK = 16
GLA_TAU = 16.0
GLA_CHUNK = 64
GLA_SUB = 16
NSA_HEADS = 16
NSA_G = 4
NSA_P = 4
NSA_DH = 64
CMP_BLOCK = 32
CMP_STRIDE = 16
SEL_BLOCK = 64
TOP_N = 16
WINDOW = 512
D_FF = 2816
KV_W = 512
EPS = 1e-6
NEG = -1e30
TINY = 1e-30
FORCE_SCORE = 1e4

COL_TILE = 512
C_MGATE = 0
C_RGY = 3072
C_RGX = 4096
C_GQ = 5120
C_GK = 5632
C_GV = 6144
C_GR = 7168
C_NQ = 8192
C_NCMP = 9216
C_NSLC = 9728
C_NWIN = 10240
C_SMALL = 10752
ZC = 11264
SMALL_W = 128
GATE_OFF = GLA_RANK

VMEM_LIMIT = 56 * 1024 * 1024


def _cp(sem, vmem=VMEM_LIMIT):
    return pltpu.CompilerParams(dimension_semantics=sem, vmem_limit_bytes=vmem)


def _nt(a, b):
    return lax.dot_general(a, b, (((1,), (1,)), ((), ())), preferred_element_type=F32)


def _tn(a, b):
    return lax.dot_general(a, b, (((0,), (0,)), ((), ())), preferred_element_type=F32)


def _dot(a, b):
    return jnp.dot(a, b, preferred_element_type=F32)


def _softplus(x):
    return jnp.maximum(x, 0.0) + jnp.log1p(jnp.exp(-jnp.abs(x)))


def _log_sigmoid(x):
    return jnp.minimum(x, 0.0) - jnp.log1p(jnp.exp(-jnp.abs(x)))


def _gelu_tanh(x):
    c = 0.7978845608028654
    return x * (0.5 * (1.0 + jnp.tanh(c * (x + 0.044715 * (x * x * x)))))


def _silu(x):
    return x * jax.nn.sigmoid(x)


def _seg_mean_sq(x, seg_ones):
    x2 = x * x
    hi = x2.astype(BF16)
    lo = (x2 - hi.astype(F32)).astype(BF16)
    outs = []
    for c in range(x.shape[1] // 128):
        sl = slice(c * 128, (c + 1) * 128)
        outs.append(_dot(hi[:, sl], seg_ones) + _dot(lo[:, sl], seg_ones))
    y = outs[0] if len(outs) == 1 else jnp.concatenate(outs, axis=1)
    return y * (1.0 / NSA_DH)


def _inproj_kernel(x_ref, g_ref, w_ref, o_ref, h_scr):
    @pl.when(pl.program_id(1) == 0)
    def _():
        x = x_ref[...]
        ms = jnp.mean(x * x, axis=-1, keepdims=True)
        h_scr[...] = (x * lax.rsqrt(ms + EPS) * g_ref[...]).astype(BF16)

    o_ref[...] = _dot(h_scr[...], w_ref[...])


def _inproj(x2d, g, w_perm):
    n = x2d.shape[0]
    tm = min(n, 1024)
    return pl.pallas_call(
        _inproj_kernel,
        grid=(n // tm, ZC // COL_TILE),
        in_specs=[
            pl.BlockSpec((tm, D_MODEL), lambda i, j: (i, 0)),
            pl.BlockSpec((1, D_MODEL), lambda i, j: (0, 0)),
            pl.BlockSpec((D_MODEL, COL_TILE), lambda i, j: (0, j)),
        ],
        out_specs=pl.BlockSpec((tm, COL_TILE), lambda i, j: (i, j)),
        out_shape=jax.ShapeDtypeStruct((n, ZC), F32),
        scratch_shapes=[pltpu.VMEM((tm, D_MODEL), BF16)],
        compiler_params=_cp(("arbitrary", "arbitrary")),
        name="inproj",
    )(x2d, g, w_perm)


def _rglru_gates(xc, wax_ref, ba, bx, lam):
    sp = _softplus(-lam)
    a_parts, u_parts = [], []
    for n in range(RG_BLOCKS):
        sl = slice(n * RG_BLOCK_W, (n + 1) * RG_BLOCK_W)
        xb = xc[:, sl]
        r = _dot(xb.astype(BF16), wax_ref[n])
        rec = jax.nn.sigmoid(r[:, :RG_BLOCK_W] + ba[:, sl])
        inp = jax.nn.sigmoid(r[:, RG_BLOCK_W:] + bx[:, sl])
        log_a = (-RG_C) * rec * sp[:, sl]
        a = jnp.exp(log_a)
        u = jnp.sqrt(-(jnp.tanh(log_a) * (a * a + 1.0))) * (inp * xb)
        a_parts.append(a)
        u_parts.append(u)
    return a_parts, u_parts


def _rglru_kernel(y_ref, x_ref, cw_ref, cb_ref, wax_ref, ba_ref, bx_ref, lam_ref,
                  o_ref, hl_ref, xp_scr, a_scr, u_scr, h_scr):
    t = pl.program_id(1)
    tt = x_ref.shape[1]

    @pl.when(t == 0)
    def _():
        xp_scr[0:8, :] = jnp.zeros((8, D_MODEL), F32)
        h_scr[...] = jnp.zeros_like(h_scr)

    x = x_ref[0]
    xp_scr[8:8 + tt, :] = x
    cw = cw_ref[...]
    xc = (cb_ref[...] + xp_scr[5:5 + tt, :] * cw[0:1] + xp_scr[6:6 + tt, :] * cw[1:2]
          + xp_scr[7:7 + tt, :] * cw[2:3] + x * cw[3:4])
    xp_scr[0:8, :] = xp_scr[tt:tt + 8, :]

    a_parts, u_parts = _rglru_gates(xc, wax_ref, ba_ref[...], bx_ref[...], lam_ref[...])
    for n in range(RG_BLOCKS):
        sl = slice(n * RG_BLOCK_W, (n + 1) * RG_BLOCK_W)
        a_scr[:, sl] = a_parts[n]
        u_scr[:, sl] = u_parts[n]

    row = lax.broadcasted_iota(jnp.int32, (8, D_MODEL), 0)

    def body(i, h):
        r0 = pl.multiple_of(i * 8, 8)
        a = a_scr[pl.ds(r0, 8), :]
        u = u_scr[pl.ds(r0, 8), :]
        for d in (1, 2, 4):
            a_sh = pltpu.roll(a, d, 0)
            u_sh = pltpu.roll(u, d, 0)
            m = row >= d
            u = jnp.where(m, a * u_sh + u, u)
            a = jnp.where(m, a * a_sh, a)
        hh = u + a * h
        u_scr[pl.ds(r0, 8), :] = hh
        return hh[7:8, :]

    h = lax.fori_loop(0, tt // 8, body, h_scr[0:1, :])
    h_scr[...] = jnp.broadcast_to(h, h_scr.shape)
    hl_ref[0] = jnp.broadcast_to(h, (8, D_MODEL))
    o_ref[0] = _gelu_tanh(y_ref[0]) * u_scr[...]


def _rglru_prompt(z3, cw, cb, wax, ba, bx, lam):
    b, t, _ = z3.shape
    tt = min(t, 512)
    row = lambda i, j: (0, 0)
    return pl.pallas_call(
        _rglru_kernel,
        grid=(b, t // tt),
        in_specs=[
            pl.BlockSpec((1, tt, D_MODEL), lambda i, j: (i, j, C_RGY // D_MODEL)),
            pl.BlockSpec((1, tt, D_MODEL), lambda i, j: (i, j, C_RGX // D_MODEL)),
            pl.BlockSpec((RG_CONV_W, D_MODEL), row),
            pl.BlockSpec((1, D_MODEL), row),
            pl.BlockSpec((RG_BLOCKS, RG_BLOCK_W, 2 * RG_BLOCK_W), lambda i, j: (0, 0, 0)),
            pl.BlockSpec((1, D_MODEL), row),
            pl.BlockSpec((1, D_MODEL), row),
            pl.BlockSpec((1, D_MODEL), row),
        ],
        out_specs=[
            pl.BlockSpec((1, tt, D_MODEL), lambda i, j: (i, j, 0)),
            pl.BlockSpec((1, 8, D_MODEL), lambda i, j: (i, 0, 0)),
        ],
        out_shape=[
            jax.ShapeDtypeStruct((b, t, D_MODEL), F32),
            jax.ShapeDtypeStruct((b, 8, D_MODEL), F32),
        ],
        scratch_shapes=[
            pltpu.VMEM((tt + 8, D_MODEL), F32),
            pltpu.VMEM((tt, D_MODEL), F32),
            pltpu.VMEM((tt, D_MODEL), F32),
            pltpu.VMEM((8, D_MODEL), F32),
        ],
        compiler_params=_cp(("arbitrary", "arbitrary")),
        name="rglru_prompt",
    )(z3, z3, cw, cb, wax, ba, bx, lam)


def _gla_chunk_head(q, k, v, lg, st):
    c = GLA_CHUNK
    r_i = lax.broadcasted_iota(jnp.int32, (c, c), 0)
    c_i = lax.broadcasted_iota(jnp.int32, (c, c), 1)
    tril = (r_i >= c_i).astype(F32)
    b = jnp.dot(tril, lg, preferred_element_type=F32, precision=lax.Precision.HIGHEST)
    kb = k.astype(BF16)
    vb = v.astype(BF16)
    o_state = _nt((q * jnp.exp(b)).astype(BF16), st.astype(BF16))
    sub = GLA_SUB
    t_row = lax.broadcasted_iota(jnp.int32, (sub, GLA_DK), 0)
    lane = lax.broadcasted_iota(jnp.int32, (sub, GLA_DK), 1)
    o_parts = []
    for blk in range(c // sub):
        r0 = blk * sub
        q_i = q[r0:r0 + sub]
        k_i = k[r0:r0 + sub]
        b_i = b[r0:r0 + sub]
        a_d = jnp.zeros((sub, GLA_DK), F32)
        for s in range(sub):
            diff = b_i - b_i[s:s + 1]
            dec = jnp.exp(jnp.where(t_row >= s, diff, NEG))
            col = jnp.sum(q_i * k_i[s:s + 1] * dec, axis=-1, keepdims=True)
            a_d = jnp.where(lane == s, col, a_d)
        o_i = _dot(a_d[:, :sub].astype(BF16), vb[r0:r0 + sub])
        if blk > 0:
            b_ref = b[r0 - 1:r0]
            qd = (q_i * jnp.exp(b_i - b_ref)).astype(BF16)
            kd = (k[:r0] * jnp.exp(b_ref - b[:r0])).astype(BF16)
            a_off = _nt(qd, kd)
            o_i = o_i + _dot(a_off.astype(BF16), vb[:r0])
        o_parts.append(o_i)
    o = o_state + jnp.concatenate(o_parts, axis=0)
    b_last = b[c - 1:c]
    kd = (k * jnp.exp(b_last - b)).astype(BF16)
    st_new = st * jnp.exp(b_last) + _tn(vb, kd)
    del kb
    return o, st_new


def _gla_kernel(q_ref, k_ref, v_ref, r_ref, ga_ref, wa_ref, ba_ref, gn_ref,
                o_ref, st_ref, st_scr):
    t = pl.program_id(1)
    tt = q_ref.shape[1]

    @pl.when(t == 0)
    def _():
        st_scr[...] = jnp.zeros_like(st_scr)

    gn = gn_ref[...]

    def body(ci, carry):
        r0 = pl.multiple_of(ci * GLA_CHUNK, GLA_CHUNK)
        rows = pl.ds(r0, GLA_CHUNK)
        xg = _dot(ga_ref[0, rows, :].astype(BF16), wa_ref[...]) + ba_ref[...]
        lg_all = _log_sigmoid(xg) * (1.0 / GLA_TAU)
        for h in range(GLA_HEADS):
            ksl = slice(h * GLA_DK, (h + 1) * GLA_DK)
            vsl = slice(h * GLA_DV, (h + 1) * GLA_DV)
            q = q_ref[0, rows, ksl] * (GLA_DK ** -0.5)
            k = k_ref[0, rows, ksl]
            v = v_ref[0, rows, vsl]
            o, st_new = _gla_chunk_head(q, k, v, lg_all[:, ksl], st_scr[h])
            st_scr[h] = st_new
            ms = jnp.mean(o * o, axis=-1, keepdims=True)
            on = o * lax.rsqrt(ms + EPS) * gn
            o_ref[0, rows, vsl] = on * _silu(r_ref[0, rows, vsl])
        return carry

    lax.fori_loop(0, tt // GLA_CHUNK, body, 0)
    st_ref[0] = st_scr[...]


def _gla_prompt(z3, wa_pad, ba, gn):
    b, t, _ = z3.shape
    tt = min(t, 256)
    hk = GLA_HEADS * GLA_DK
    hv = GLA_HEADS * GLA_DV
    row = lambda i, j: (0, 0)
    return pl.pallas_call(
        _gla_kernel,
        grid=(b, t // tt),
        in_specs=[
            pl.BlockSpec((1, tt, hk), lambda i, j: (i, j, C_GQ // hk)),
            pl.BlockSpec((1, tt, hk), lambda i, j: (i, j, C_GK // hk)),
            pl.BlockSpec((1, tt, hv), lambda i, j: (i, j, C_GV // hv)),
            pl.BlockSpec((1, tt, hv), lambda i, j: (i, j, C_GR // hv)),
            pl.BlockSpec((1, tt, SMALL_W), lambda i, j: (i, j, C_SMALL // SMALL_W)),
            pl.BlockSpec((SMALL_W, hk), row),
            pl.BlockSpec((1, hk), row),
            pl.BlockSpec((1, GLA_DV), row),
        ],
        out_specs=[
            pl.BlockSpec((1, tt, hv), lambda i, j: (i, j, 0)),
            pl.BlockSpec((1, GLA_HEADS, GLA_DV, GLA_DK), lambda i, j: (i, 0, 0, 0)),
        ],
        out_shape=[
            jax.ShapeDtypeStruct((b, t, hv), F32),
            jax.ShapeDtypeStruct((b, GLA_HEADS, GLA_DV, GLA_DK), F32),
        ],
        scratch_shapes=[pltpu.VMEM((GLA_HEADS, GLA_DV, GLA_DK), F32)],
        compiler_params=_cp(("arbitrary", "arbitrary")),
        name="gla_prompt",
    )(z3, z3, z3, z3, z3, wa_pad, ba, gn)


def _nsa_prep_kernel(nq_ref, slc_ref, win_ref, sm_ref, qg_ref, kg_ref, seg_ref,
                     q_ref, slc_o, ks_ref, vs_ref, win_o, kw_ref, vw_ref, gt_ref):
    seg = seg_ref[...]
    half = NSA_G * NSA_DH
    x = nq_ref[0]
    qn = x * lax.rsqrt(_seg_mean_sq(x, seg) + EPS) * qg_ref[...] * (NSA_DH ** -0.5)
    for h in range(NSA_HEADS):
        q_ref[0, h] = qn[:, h * NSA_DH:(h + 1) * NSA_DH].astype(BF16)
    for idx, (src, dst, kh, vh) in enumerate(((slc_ref, slc_o, ks_ref, vs_ref),
                                              (win_ref, win_o, kw_ref, vw_ref))):
        kv = src[0]
        kk = kv[:, :half]
        vv = kv[:, half:]
        kn = kk * lax.rsqrt(_seg_mean_sq(kk, seg) + EPS) * kg_ref[idx + 1:idx + 2, :]
        dst[0, :, :half] = kn
        dst[0, :, half:] = vv
        for g in range(NSA_G):
            sl = slice(g * NSA_DH, (g + 1) * NSA_DH)
            kh[0, g] = kn[:, sl].astype(BF16)
            vh[0, g] = vv[:, sl].astype(BF16)
    sig = jax.nn.sigmoid(sm_ref[0])
    for g in range(NSA_G):
        lo = GATE_OFF + g * NSA_P * 3
        gt_ref[0, g] = sig[:, lo:lo + NSA_P * 3]


def _nsa_prep(z3, qg_t, kg_t, seg_ones):
    b, t, _ = z3.shape
    tt = min(t, 512)
    row = lambda i, j: (0, 0)
    hm_q = pl.BlockSpec((1, NSA_HEADS, tt, NSA_DH), lambda i, j: (i, 0, j, 0))
    hm_kv = pl.BlockSpec((1, NSA_G, tt, NSA_DH), lambda i, j: (i, 0, j, 0))
    flat = pl.BlockSpec((1, tt, KV_W), lambda i, j: (i, j, 0))
    sd = jax.ShapeDtypeStruct
    return pl.pallas_call(
        _nsa_prep_kernel,
        grid=(b, t // tt),
        in_specs=[
            pl.BlockSpec((1, tt, D_MODEL), lambda i, j: (i, j, C_NQ // D_MODEL)),
            pl.BlockSpec((1, tt, KV_W), lambda i, j: (i, j, C_NSLC // KV_W)),
            pl.BlockSpec((1, tt, KV_W), lambda i, j: (i, j, C_NWIN // KV_W)),
            pl.BlockSpec((1, tt, SMALL_W), lambda i, j: (i, j, C_SMALL // SMALL_W)),
            pl.BlockSpec((1, D_MODEL), row),
            pl.BlockSpec((3, NSA_G * NSA_DH), row),
            pl.BlockSpec((128, 128), row),
        ],
        out_specs=[hm_q, flat, hm_kv, hm_kv, flat, hm_kv, hm_kv,
                   pl.BlockSpec((1, NSA_G, tt, NSA_P * 3), lambda i, j: (i, 0, j, 0))],
        out_shape=[
            sd((b, NSA_HEADS, t, NSA_DH), BF16),
            sd((b, t, KV_W), F32),
            sd((b, NSA_G, t, NSA_DH), BF16),
            sd((b, NSA_G, t, NSA_DH), BF16),
            sd((b, t, KV_W), F32),
            sd((b, NSA_G, t, NSA_DH), BF16),
            sd((b, NSA_G, t, NSA_DH), BF16),
            sd((b, NSA_G, t, NSA_P * 3), F32),
        ],
        compiler_params=_cp(("arbitrary", "arbitrary")),
        name="nsa_prep",
    )(z3, z3, z3, z3, qg_t, kg_t, seg_ones)


def _compress_half(load_rows, nch, ph_ref, pe_ref, sh_scr):
    half = NSA_G * NSA_DH
    lo = jnp.zeros((nch, half), F32)
    hi = jnp.zeros((nch, half), F32)
    for s in range(CMP_STRIDE):
        x = load_rows(s)
        lo += _dot((x + pe_ref[s:s + 1, :]).astype(BF16), ph_ref[s])
        hi += _dot((x + pe_ref[CMP_STRIDE + s:CMP_STRIDE + s + 1, :]).astype(BF16),
                   ph_ref[CMP_STRIDE + s])
    sh_scr[0:nch, :] = hi
    sh_scr[nch:nch + 8, :] = jnp.zeros((8, half), F32)
    return lo + sh_scr[1:nch + 1, :]


def _compress_kernel(x_ref, phk_ref, phv_ref, pek_ref, pev_ref, kg_ref, seg_ref,
                     kc_ref, vc_ref, sh_scr):
    nch = x_ref.shape[1]
    half = NSA_G * NSA_DH
    k = _compress_half(lambda s: x_ref[0, :, s * KV_W:s * KV_W + half], nch,
                       phk_ref, pek_ref, sh_scr)
    v = _compress_half(lambda s: x_ref[0, :, s * KV_W + half:(s + 1) * KV_W], nch,
                       phv_ref, pev_ref, sh_scr)
    kn = k * lax.rsqrt(_seg_mean_sq(k, seg_ref[...]) + EPS) * kg_ref[0:1, :]
    for g in range(NSA_G):
        sl = slice(g * NSA_DH, (g + 1) * NSA_DH)
        kc_ref[0, g] = kn[:, sl].astype(BF16)
        vc_ref[0, g] = v[:, sl].astype(BF16)


def _compress_prompt(cmp_rows, phk, phv, pek, pev, kg_t, seg_ones):
    b, nch, _ = cmp_rows.shape
    half = NSA_G * NSA_DH
    c3 = lambda i: (0, 0, 0)
    c2 = lambda i: (0, 0)
    hm = pl.BlockSpec((1, NSA_G, nch, NSA_DH), lambda i: (i, 0, 0, 0))
    return pl.pallas_call(
        _compress_kernel,
        grid=(b,),
        in_specs=[
            pl.BlockSpec((1, nch, CMP_STRIDE * KV_W), lambda i: (i, 0, 0)),
            pl.BlockSpec((CMP_BLOCK, half, half), c3),
            pl.BlockSpec((CMP_BLOCK, half, half), c3),
            pl.BlockSpec((CMP_BLOCK, half), c2),
            pl.BlockSpec((CMP_BLOCK, half), c2),
            pl.BlockSpec((3, half), c2),
            pl.BlockSpec((128, 128), c2),
        ],
        out_specs=[hm, hm],
        out_shape=[jax.ShapeDtypeStruct((b, NSA_G, nch, NSA_DH), BF16)] * 2,
        scratch_shapes=[pltpu.VMEM((nch + 8, half), F32)],
        compiler_params=_cp(("arbitrary",)),
        name="compress_prompt",
    )(cmp_rows, phk, phv, pek, pev, kg_t, seg_ones)


def _gated_heads(o, gates, c):
    qb = o.shape[0] // NSA_P
    parts = []
    for p in range(NSA_P):
        gcol = gates[:, p * 3 + c:p * 3 + c + 1]
        parts.append(o[p * qb:(p + 1) * qb] * gcol)
    return jnp.concatenate(parts, axis=1)


def _rank_select(score_t, n_keep):
    nj, nq = score_t.shape
    grp = 8
    tiles = [score_t[g * grp:(g + 1) * grp] for g in range(nj // grp)]
    ranks = [jnp.zeros((grp, nq), F32) for _ in tiles]
    jrow = lax.broadcasted_iota(jnp.int32, (grp, nq), 0)
    for jp in range(nj):
        r = score_t[jp:jp + 1, :]
        for g, s_g in enumerate(tiles):
            if g * grp > jp:
                ahead = jnp.where(r >= s_g, 1.0, 0.0)
            elif (g + 1) * grp - 1 <= jp:
                ahead = jnp.where(r > s_g, 1.0, 0.0)
            else:
                ahead = jnp.where(jrow + g * grp > jp, jnp.where(r >= s_g, 1.0, 0.0),
                                  jnp.where(r > s_g, 1.0, 0.0))
            ranks[g] = ranks[g] + ahead
    rank = jnp.concatenate(ranks, axis=0)
    return jnp.where(rank < n_keep, 1.0, 0.0)


def _cmp_attn_kernel(q_ref, kc_ref, vc_ref, gt_ref, sl_ref, m2s_ref,
                     o_ref, sel_ref, act_ref, *, n_keep):
    i = pl.program_id(2)
    qb = q_ref.shape[2]
    rows = NSA_P * qb
    nc = kc_ref.shape[2]
    q = q_ref[0].reshape(rows, NSA_DH)
    s = _nt(q, kc_ref[0, 0])
    r_i = lax.broadcasted_iota(jnp.int32, (rows, nc), 0)
    q_pos = i * qb + r_i % qb
    c_end = lax.broadcasted_iota(jnp.int32, (rows, nc), 1) * CMP_STRIDE + (CMP_BLOCK - 1)
    mask = c_end <= q_pos
    dist = (q_pos - c_end).astype(F32)
    sc = jnp.where(mask, s - sl_ref[0] * dist, NEG)
    m = jnp.max(sc, axis=-1, keepdims=True)
    e = jnp.where(mask, jnp.exp(sc - m), 0.0)
    p = e / jnp.maximum(jnp.sum(e, axis=-1, keepdims=True), TINY)
    o = _dot(p.astype(BF16), vc_ref[0, 0])
    o_ref[0] = _gated_heads(o, gt_ref[0, 0], 0)
    ps = p[0:qb]
    for pp in range(1, NSA_P):
        ps = ps + p[pp * qb:(pp + 1) * qb]
    hi = ps.astype(BF16)
    lo = (ps - hi.astype(F32)).astype(BF16)
    imp_t = _nt(m2s_ref[...], hi) + _nt(m2s_ref[...], lo)
    nj = imp_t.shape[0]
    jrow = lax.broadcasted_iota(jnp.int32, (nj, qb), 0)
    valid = jrow <= i
    forced = (jrow == 0) | (jrow == i) | (jrow == i - 1)
    score_t = jnp.where(valid, jnp.where(forced, FORCE_SCORE, imp_t), NEG)
    sel_t = _rank_select(score_t, n_keep)
    eye = (lax.broadcasted_iota(jnp.int32, (qb, qb), 0)
           == lax.broadcasted_iota(jnp.int32, (qb, qb), 1)).astype(BF16)
    sel_ref[0, 0] = _nt(eye, sel_t.astype(BF16))
    n_kt = act_ref.shape[3]
    in_tile = (lax.broadcasted_iota(jnp.int32, (n_kt, nj), 1) // (KEY_TILE // SEL_BLOCK)
               == lax.broadcasted_iota(jnp.int32, (n_kt, nj), 0)).astype(BF16)
    picks = jnp.max(_dot(in_tile, sel_t.astype(BF16)), axis=-1, keepdims=True)
    act_ref[0, 0, 0] = jnp.broadcast_to(jnp.where(picks > 0.5, 1, 0), (n_kt, 128))


def _cmp_attn_prompt(q_hm, kc, vc, gates, slope_col, m2s_t):
    b, _, t, _ = q_hm.shape
    qb = SEL_BLOCK
    nqb = t // qb
    nc = kc.shape[2]
    nsel = m2s_t.shape[0]
    n_kt = t // KEY_TILE
    return pl.pallas_call(
        functools.partial(_cmp_attn_kernel, n_keep=min(TOP_N, nsel)),
        grid=(b, NSA_G, nqb),
        in_specs=[
            pl.BlockSpec((1, NSA_P, qb, NSA_DH), lambda bi, g, i: (bi, g, i, 0)),
            pl.BlockSpec((1, 1, nc, NSA_DH), lambda bi, g, i: (bi, g, 0, 0)),
            pl.BlockSpec((1, 1, nc, NSA_DH), lambda bi, g, i: (bi, g, 0, 0)),
            pl.BlockSpec((1, 1, qb, NSA_P * 3), lambda bi, g, i: (bi, g, i, 0)),
            pl.BlockSpec((1, NSA_P * qb, 1), lambda bi, g, i: (g, 0, 0)),
            pl.BlockSpec((nsel, nc), lambda bi, g, i: (0, 0)),
        ],
        out_specs=[
            pl.BlockSpec((1, qb, NSA_P * NSA_DH), lambda bi, g, i: (bi, i, g)),
            pl.BlockSpec((1, 1, qb, nsel), lambda bi, g, i: (bi, g, i, 0)),
            pl.BlockSpec((1, 1, 1, n_kt, 128), lambda bi, g, i: (bi, g, i, 0, 0)),
        ],
        out_shape=[
            jax.ShapeDtypeStruct((b, t, D_MODEL), F32),
            jax.ShapeDtypeStruct((b, NSA_G, t, nsel), F32),
            jax.ShapeDtypeStruct((b, NSA_G, nqb, n_kt, 128), jnp.int32),
        ],
        compiler_params=_cp(("arbitrary", "arbitrary", "arbitrary")),
        name="cmp_attn_prompt",
    )(q_hm, kc, vc, gates, slope_col, m2s_t)


KEY_TILE = 512
WIN_QB = 128
WIN_SPAN = WINDOW + WIN_QB


def _slc_attn_kernel(act_ref, q_ref, k_ref, v_ref, gt_ref, sl_ref, sel_ref, ex_ref, o_ref,
                     sc_scr):
    i = pl.program_id(2)
    qb = q_ref.shape[2]
    step = (pl.program_id(0) * pl.num_programs(1) + pl.program_id(1)) * pl.num_programs(2) + i
    act_base = step * sc_scr.shape[0]
    sel_bias = ((sel_ref[0, 0] - 1.0) * (-NEG)).astype(BF16)
    q_abs = i * qb + lax.broadcasted_iota(jnp.int32, (qb, KEY_TILE), 0)
    lane = lax.broadcasted_iota(jnp.int32, (qb, KEY_TILE), 1)
    lane_row = lax.broadcasted_iota(jnp.int32, (1, KEY_TILE), 1)
    q_all = q_ref[0].reshape(NSA_P * qb, NSA_DH)
    slopes = [sl_ref[0, p * qb:p * qb + 1, :] for p in range(NSA_P)]
    n_tiles = (i * qb) // KEY_TILE + 1

    def tile_scores(kt, ms):
        k0 = pl.multiple_of(kt * KEY_TILE, KEY_TILE)
        kk = k_ref[0, 0, pl.ds(k0, KEY_TILE), :]
        mb = _dot(sel_bias, ex_ref[kt]) + jnp.where(k0 + lane > q_abs, NEG, 0.0)
        rel = (k0 - i * qb + lane_row).astype(F32)
        s_all = _nt(q_all, kk)
        out = []
        for p in range(NSA_P):
            sc = s_all[p * qb:(p + 1) * qb] + (mb + slopes[p] * rel)
            sc_scr[kt, p] = sc
            out.append(jnp.maximum(ms[p], jnp.max(sc, axis=-1, keepdims=True)))
        return tuple(out)

    def scores(kt, ms):
        return lax.cond(act_ref[act_base + kt] != 0, functools.partial(tile_scores, kt),
                        lambda ms: ms, ms)

    ms = lax.fori_loop(0, n_tiles, scores,
                       tuple(jnp.full((qb, 1), NEG, F32) for _ in range(NSA_P)))

    def tile_weigh(kt, carry):
        k0 = pl.multiple_of(kt * KEY_TILE, KEY_TILE)
        vv = v_ref[0, 0, pl.ds(k0, KEY_TILE), :]
        lsums, acc = carry
        out, es = [], []
        for p in range(NSA_P):
            e = jnp.exp(sc_scr[kt, p] - ms[p])
            lsum = lsums[p]
            for c in range(KEY_TILE // 128):
                lsum = lsum + e[:, c * 128:(c + 1) * 128]
            out.append(lsum)
            es.append(e.astype(BF16))
        return tuple(out), acc + _dot(jnp.concatenate(es, axis=0), vv)

    def weigh(kt, carry):
        return lax.cond(act_ref[act_base + kt] != 0, functools.partial(tile_weigh, kt),
                        lambda c: c, carry)

    init = (tuple(jnp.zeros((qb, 128), F32) for _ in range(NSA_P)),
            jnp.zeros((NSA_P * qb, NSA_DH), F32))
    lsums, acc = lax.fori_loop(0, n_tiles, weigh, init)
    l = jnp.concatenate([jnp.sum(ls, axis=-1, keepdims=True) for ls in lsums], axis=0)
    o_ref[0] = _gated_heads(acc / jnp.maximum(l, TINY), gt_ref[0, 0], 1)


def _win_attn_kernel(q_ref, k_ref, v_ref, gt_ref, sl_ref, o_ref):
    i = pl.program_id(2)
    qb = q_ref.shape[2]
    k0 = pl.multiple_of(jnp.maximum(i * qb - WINDOW, 0), 128)
    kk = k_ref[0, 0, pl.ds(k0, WIN_SPAN), :]
    vv = v_ref[0, 0, pl.ds(k0, WIN_SPAN), :]
    q_abs = i * qb + lax.broadcasted_iota(jnp.int32, (qb, WIN_SPAN), 0)
    dist = q_abs - (k0 + lax.broadcasted_iota(jnp.int32, (qb, WIN_SPAN), 1))
    mb = jnp.where((dist >= 0) & (dist <= WINDOW), 0.0, NEG)
    rel = (k0 - i * qb + lax.broadcasted_iota(jnp.int32, (1, WIN_SPAN), 1)).astype(F32)
    s_all = _nt(q_ref[0].reshape(NSA_P * qb, NSA_DH), kk)
    es, ls = [], []
    for p in range(NSA_P):
        sc = s_all[p * qb:(p + 1) * qb] + (mb + sl_ref[0, p * qb:p * qb + 1, :] * rel)
        m = jnp.max(sc, axis=-1, keepdims=True)
        e = jnp.exp(sc - m)
        ls.append(jnp.sum(e, axis=-1, keepdims=True))
        es.append(e.astype(BF16))
    o = _dot(jnp.concatenate(es, axis=0), vv) / jnp.maximum(jnp.concatenate(ls, axis=0), TINY)
    o_ref[0] = _gated_heads(o, gt_ref[0, 0], 2)


def _branch_attn_prompt(q_hm, k_hm, v_hm, gates, slope_col, sel=None, expand=None, active=None):
    b, _, t, _ = q_hm.shape
    qb = WIN_QB if sel is None else SEL_BLOCK
    nqb = t // qb
    in_specs = [
        pl.BlockSpec((1, NSA_P, qb, NSA_DH), lambda bi, g, i, *_: (bi, g, i, 0)),
        pl.BlockSpec((1, 1, t, NSA_DH), lambda bi, g, i, *_: (bi, g, 0, 0)),
        pl.BlockSpec((1, 1, t, NSA_DH), lambda bi, g, i, *_: (bi, g, 0, 0)),
        pl.BlockSpec((1, 1, qb, NSA_P * 3), lambda bi, g, i, *_: (bi, g, i, 0)),
        pl.BlockSpec((1, NSA_P * qb, 1), lambda bi, g, i, *_: (g, 0, 0)),
    ]
    args = [q_hm, k_hm, v_hm, gates, slope_col]
    scratch, prefetch = [], []
    if sel is None:
        body, name = _win_attn_kernel, "attn_win"
    else:
        scratch = [pltpu.VMEM((t // KEY_TILE, NSA_P, qb, KEY_TILE), F32)]
        nsel = sel.shape[-1]
        in_specs += [
            pl.BlockSpec((1, 1, qb, nsel), lambda bi, g, i, *_: (bi, g, i, 0)),
            pl.BlockSpec((expand.shape[0], nsel, KEY_TILE), lambda bi, g, i, *_: (0, 0, 0)),
        ]
        args += [sel, expand]
        prefetch = [active]
        body, name = _slc_attn_kernel, "attn_slc"
    grid_spec = pltpu.PrefetchScalarGridSpec(
        num_scalar_prefetch=len(prefetch),
        grid=(b, NSA_G, nqb),
        in_specs=in_specs,
        out_specs=pl.BlockSpec((1, qb, NSA_P * NSA_DH), lambda bi, g, i, *_: (bi, i, g)),
        scratch_shapes=scratch,
    )
    return pl.pallas_call(
        body,
        grid_spec=grid_spec,
        out_shape=jax.ShapeDtypeStruct((b, t, D_MODEL), F32),
        compiler_params=_cp(("arbitrary", "arbitrary", "arbitrary")),
        name=name,
    )(*prefetch, *args)


def _merge_kernel(x_ref, oa_ref, ob_ref, c1_ref, c2_ref, c3_ref, ma_ref, mb_ref, mc_ref,
                  wa_ref, wb_ref, wc_ref, wo_ref, o_ref):
    oc = c1_ref[...] + c2_ref[...] + c3_ref[...]
    merged = (jax.nn.sigmoid(ma_ref[...]) * _dot(oa_ref[...].astype(BF16), wa_ref[...])
              + jax.nn.sigmoid(mb_ref[...]) * _dot(ob_ref[...].astype(BF16), wb_ref[...])
              + jax.nn.sigmoid(mc_ref[...]) * _dot(oc.astype(BF16), wc_ref[...]))
    o_ref[...] = x_ref[...] + _dot(merged.astype(BF16), wo_ref[...])


def _merge(x2d, z2d, oa, ob, c1, c2, c3, wa, wb, wc, wo):
    n = x2d.shape[0]
    tt = min(n, 256)
    tok = pl.BlockSpec((tt, D_MODEL), lambda i: (i, 0))
    wsp = pl.BlockSpec((D_MODEL, D_MODEL), lambda i: (0, 0))
    zsp = [pl.BlockSpec((tt, D_MODEL), functools.partial(lambda i, c: (i, c), c=C_MGATE // D_MODEL + c))
           for c in range(3)]
    return pl.pallas_call(
        _merge_kernel,
        grid=(n // tt,),
        in_specs=[tok] * 6 + zsp + [wsp] * 4,
        out_specs=tok,
        out_shape=jax.ShapeDtypeStruct((n, D_MODEL), F32),
        compiler_params=_cp(("arbitrary",)),
        name="merge",
    )(x2d, oa, ob, c1, c2, c3, z2d, z2d, z2d, wa, wb, wc, wo)


FF_TILE = 1408


def _ffn_kernel(x_ref, g_ref, wg_ref, wu_ref, wd_ref, o_ref, h_scr, acc_scr):
    j = pl.program_id(1)

    @pl.when(j == 0)
    def _():
        x = x_ref[...]
        ms = jnp.mean(x * x, axis=-1, keepdims=True)
        h_scr[...] = (x * lax.rsqrt(ms + EPS) * g_ref[...]).astype(BF16)
        acc_scr[...] = x

    h = h_scr[...]
    act = _silu(_dot(h, wg_ref[...])) * _dot(h, wu_ref[...])
    acc_scr[...] += _dot(act.astype(BF16), wd_ref[...])

    @pl.when(j == pl.num_programs(1) - 1)
    def _():
        o_ref[...] = acc_scr[...]


def _ffn(x2d, g, wg, wu, wd):
    n = x2d.shape[0]
    tt = min(n, 512)
    return pl.pallas_call(
        _ffn_kernel,
        grid=(n // tt, D_FF // FF_TILE),
        in_specs=[
            pl.BlockSpec((tt, D_MODEL), lambda i, j: (i, 0)),
            pl.BlockSpec((1, D_MODEL), lambda i, j: (0, 0)),
            pl.BlockSpec((D_MODEL, FF_TILE), lambda i, j: (0, j)),
            pl.BlockSpec((D_MODEL, FF_TILE), lambda i, j: (0, j)),
            pl.BlockSpec((FF_TILE, D_MODEL), lambda i, j: (j, 0)),
        ],
        out_specs=pl.BlockSpec((tt, D_MODEL), lambda i, j: (i, 0)),
        out_shape=jax.ShapeDtypeStruct((n, D_MODEL), F32),
        scratch_shapes=[pltpu.VMEM((tt, D_MODEL), BF16), pltpu.VMEM((tt, D_MODEL), F32)],
        compiler_params=_cp(("arbitrary", "arbitrary")),
        name="ffn",
    )(x2d, g, wg, wu, wd)


IN_SIZES = (1024, 1024, 512, 512, 1024, 1024, 16, 1024, 512, 512, 512, 48, 3072)


def _perm_w_in(w):
    parts, start = [], 0
    for n in IN_SIZES:
        parts.append(w[:, start:start + n])
        start += n
    (rg_y, rg_x, g_q, g_k, g_v, g_r, g_a, n_q, n_cmp, n_slc, n_win, n_gate, m_gate) = parts
    pad = jnp.zeros((w.shape[0], ZC - C_SMALL - GLA_RANK - 3 * NSA_HEADS), w.dtype)
    return jnp.concatenate([m_gate, rg_y, rg_x, g_q, g_k, g_v, g_r, n_q, n_cmp, n_slc, n_win,
                            g_a, n_gate, pad], axis=1).astype(BF16)


def _block_diag4(phi):
    eye = jnp.eye(NSA_G, dtype=phi.dtype)
    out = jnp.einsum('gh,sde->sgdhe', eye, phi)
    return out.reshape(phi.shape[0], NSA_G * NSA_DH, NSA_G * NSA_DH)


def _alibi_slopes():
    h = jnp.arange(1, NSA_HEADS + 1, dtype=F32)
    return (2.0 ** (-8.0 * h / NSA_HEADS)).reshape(NSA_G, NSA_P)


def _layer_params(l, norm_mix_g, w_in, rg_conv_w, rg_conv_b, rg_w_a, rg_b_a, rg_w_x, rg_b_x,
                  rg_lambda, gla_w_a, gla_b_a, gla_norm_g, nsa_q_norm_g, nsa_k_norm_g, nsa_phi_k,
                  nsa_phi_v, nsa_cmp_pe, w_rg_out, w_gla_out, w_nsa_out, w_out, norm_ffn_g,
                  w_ffn_gate, w_ffn_up, w_ffn_down):
    half = NSA_G * NSA_DH
    p = {}
    p['norm_mix'] = norm_mix_g[l][None, :]
    p['w_in'] = _perm_w_in(w_in[l])
    p['cw'] = rg_conv_w[l]
    p['cb'] = rg_conv_b[l][None, :]
    p['wax'] = jnp.concatenate([rg_w_a[l], rg_w_x[l]], axis=-1).astype(BF16)
    p['ba'] = rg_b_a[l][None, :]
    p['bx'] = rg_b_x[l][None, :]
    p['lam'] = rg_lambda[l][None, :]
    p['gla_wa'] = jnp.pad(gla_w_a[l], ((0, SMALL_W - GLA_RANK), (0, 0))).astype(BF16)
    p['gla_ba'] = gla_b_a[l][None, :]
    p['gla_gn'] = gla_norm_g[l][None, :]
    p['qg'] = jnp.tile(nsa_q_norm_g[l], NSA_HEADS)[None, :]
    p['kg'] = jnp.tile(nsa_k_norm_g[l], (1, NSA_G))
    p['phk'] = _block_diag4(nsa_phi_k[l]).astype(BF16)
    p['phv'] = _block_diag4(nsa_phi_v[l]).astype(BF16)
    p['pek'] = jnp.tile(nsa_cmp_pe[l, 0], (1, NSA_G))
    p['pev'] = jnp.tile(nsa_cmp_pe[l, 1], (1, NSA_G))
    p['w_rg_out'] = w_rg_out[l].astype(BF16)
    p['w_gla_out'] = w_gla_out[l].astype(BF16)
    p['w_nsa_out'] = w_nsa_out[l].astype(BF16)
    p['w_out'] = w_out[l].astype(BF16)
    p['norm_ffn'] = norm_ffn_g[l][None, :]
    p['wg'] = w_ffn_gate[l].astype(BF16)
    p['wu'] = w_ffn_up[l].astype(BF16)
    p['wd'] = w_ffn_down[l].astype(BF16)
    del half
    return p


def _seg_ones():
    r = np.arange(128) // NSA_DH
    return jnp.asarray(r[:, None] == r[None, :], BF16)


def _cmp_to_sel_t(n_cmp_rows, n_cmp_valid, n_sel):
    start = np.arange(n_cmp_rows)[None, :] * CMP_STRIDE
    sel = np.arange(n_sel)[:, None] * SEL_BLOCK
    ok = ((start < sel + SEL_BLOCK) & (start + CMP_BLOCK > sel)
          & (np.arange(n_cmp_rows)[None, :] < n_cmp_valid))
    return jnp.asarray(ok, BF16)


def _expand_blocks(n_sel, n_keys):
    key_blk = np.arange(n_keys) // SEL_BLOCK
    e = np.arange(n_sel)[:, None] == key_blk[None, :]
    e = e.reshape(n_sel, n_keys // KEY_TILE, KEY_TILE).transpose(1, 0, 2)
    return jnp.asarray(e, BF16)


def _slope_cols(rows_per_head):
    return jnp.repeat(_alibi_slopes(), rows_per_head, axis=1)[:, :, None]


def _prompt_layer(x, p):
    b, t, _ = x.shape
    n = b * t
    z2 = _inproj(x.reshape(n, D_MODEL), p['norm_mix'], p['w_in'])
    z3 = z2.reshape(b, t, ZC)
    o_a, h_last = _rglru_prompt(z3, p['cw'], p['cb'], p['wax'], p['ba'], p['bx'], p['lam'])
    o_b, st = _gla_prompt(z3, p['gla_wa'], p['gla_ba'], p['gla_gn'])
    seg = _seg_ones()
    q_hm, slc_n, ks, vs, win_n, kw, vw, gates = _nsa_prep(z3, p['qg'], p['kg'], seg)
    nch = t // CMP_STRIDE
    cmp_flat = z3[:, :, C_NCMP:C_NCMP + KV_W]
    kc, vc = _compress_prompt(cmp_flat.reshape(b, nch, CMP_STRIDE * KV_W), p['phk'], p['phv'],
                              p['pek'], p['pev'], p['kg'], seg)
    n_sel = t // SEL_BLOCK
    m2s_t = _cmp_to_sel_t(nch, nch - 1, n_sel)
    slope_col = _slope_cols(SEL_BLOCK)
    o_cmp, sel, act = _cmp_attn_prompt(q_hm, kc, vc, gates, slope_col, m2s_t)
    expand = _expand_blocks(n_sel, t)
    o_slc = _branch_attn_prompt(q_hm, ks, vs, gates, slope_col, sel, expand,
                                act[..., 0].reshape(-1))
    o_win = _branch_attn_prompt(q_hm, kw, vw, gates, _slope_cols(WIN_QB))
    x2 = _merge(x.reshape(n, D_MODEL), z2, o_a.reshape(n, D_MODEL), o_b.reshape(n, D_MODEL),
                o_cmp.reshape(n, D_MODEL), o_slc.reshape(n, D_MODEL), o_win.reshape(n, D_MODEL),
                p['w_rg_out'], p['w_gla_out'], p['w_nsa_out'], p['w_out'])
    y = _ffn(x2, p['norm_ffn'], p['wg'], p['wu'], p['wd']).reshape(b, t, D_MODEL)
    kv_shape = (b, t, 2, NSA_G, NSA_DH)
    cmp_kv = cmp_flat.reshape(kv_shape)
    slc_kv = slc_n.reshape(kv_shape)
    win_state = win_n[:, t - min(WINDOW, t):].reshape(b, min(WINDOW, t), 2, NSA_G, NSA_DH)
    s_new = jnp.swapaxes(st, -1, -2)
    conv_new = z3[:, t - (RG_CONV_W - 1):, C_RGX:C_RGX + D_MODEL]
    return y, (cmp_kv, slc_kv, win_state, s_new, h_last[:, 7], conv_new)


def _rglru_step_kernel(y_ref, x_ref, c0_ref, c1_ref, c2_ref, h0_ref, cw_ref, cb_ref, wax_ref,
                       ba_ref, bx_ref, lam_ref, o_ref, h_ref):
    cw = cw_ref[...]
    x = x_ref[...]
    xc = (cb_ref[...] + c0_ref[...] * cw[0:1] + c1_ref[...] * cw[1:2]
          + c2_ref[...] * cw[2:3] + x * cw[3:4])
    a_parts, u_parts = _rglru_gates(xc, wax_ref, ba_ref[...], bx_ref[...], lam_ref[...])
    a = jnp.concatenate(a_parts, axis=1)
    u = jnp.concatenate(u_parts, axis=1)
    h = u + a * h0_ref[...]
    h_ref[...] = h
    o_ref[...] = _gelu_tanh(y_ref[...]) * h


def _rglru_sample(z2, conv_state, h0, cw, cb, wax, ba, bx, lam):
    n = z2.shape[0]
    tok = lambda c: pl.BlockSpec((n, D_MODEL), functools.partial(lambda i, c: (0, c), c=c))
    full = pl.BlockSpec((n, D_MODEL), lambda i: (0, 0))
    row = pl.BlockSpec((1, D_MODEL), lambda i: (0, 0))
    return pl.pallas_call(
        _rglru_step_kernel,
        grid=(1,),
        in_specs=[tok(C_RGY // D_MODEL), tok(C_RGX // D_MODEL), full, full, full, full,
                  pl.BlockSpec((RG_CONV_W, D_MODEL), lambda i: (0, 0)), row,
                  pl.BlockSpec((RG_BLOCKS, RG_BLOCK_W, 2 * RG_BLOCK_W), lambda i: (0, 0, 0)),
                  row, row, row],
        out_specs=[full, full],
        out_shape=[jax.ShapeDtypeStruct((n, D_MODEL), F32)] * 2,
        compiler_params=_cp(("arbitrary",)),
        name="rglru_sample",
    )(z2, z2, conv_state[:, 0], conv_state[:, 1], conv_state[:, 2], h0, cw, cb, wax, ba, bx, lam)


def _row_to_col(row):
    n = row.shape[1]
    eye = lax.broadcasted_iota(jnp.int32, (n, n), 0) == lax.broadcasted_iota(jnp.int32, (n, n), 1)
    return jnp.sum(jnp.where(eye, jnp.broadcast_to(row, (n, n)), 0.0), axis=1, keepdims=True)


def _gla_step_kernel(q_ref, k_ref, v_ref, r_ref, ga_ref, s_ref, wa_ref, ba_ref, gn_ref,
                     o_ref, so_ref):
    ga = jnp.broadcast_to(ga_ref[0], (8, SMALL_W)).astype(BF16)
    xg = _dot(ga, wa_ref[...])[0:1] + ba_ref[...]
    lg = _log_sigmoid(xg) * (1.0 / GLA_TAU)
    for h in range(GLA_HEADS):
        ksl = slice(h * GLA_DK, (h + 1) * GLA_DK)
        vsl = slice(h * GLA_DV, (h + 1) * GLA_DV)
        q = q_ref[0, :, ksl] * (GLA_DK ** -0.5)
        k = k_ref[0, :, ksl]
        v = v_ref[0, :, vsl]
        b = lg[:, ksl]
        eb = jnp.exp(b)
        s_old = s_ref[0, h]
        a = jnp.sum(q * k, axis=-1, keepdims=True)
        qe = jnp.broadcast_to(q * eb, (8, GLA_DK)).astype(BF16)
        o = a * v + _dot(qe, s_old.astype(BF16))[0:1]
        so_ref[0, h] = _row_to_col(eb) * s_old + _row_to_col(k) * v
        ms = jnp.mean(o * o, axis=-1, keepdims=True)
        on = o * lax.rsqrt(ms + EPS) * gn_ref[...]
        o_ref[0, :, vsl] = on * _silu(r_ref[0, :, vsl])


def _gla_sample(z3, state, wa_pad, ba, gn):
    n = z3.shape[0]
    hk = GLA_HEADS * GLA_DK
    hv = GLA_HEADS * GLA_DV
    row = lambda i: (0, 0)
    st = pl.BlockSpec((1, GLA_HEADS, GLA_DK, GLA_DV), lambda i: (i, 0, 0, 0))
    return pl.pallas_call(
        _gla_step_kernel,
        grid=(n,),
        in_specs=[
            pl.BlockSpec((1, 1, hk), lambda i: (i, 0, C_GQ // hk)),
            pl.BlockSpec((1, 1, hk), lambda i: (i, 0, C_GK // hk)),
            pl.BlockSpec((1, 1, hv), lambda i: (i, 0, C_GV // hv)),
            pl.BlockSpec((1, 1, hv), lambda i: (i, 0, C_GR // hv)),
            pl.BlockSpec((1, 1, SMALL_W), lambda i: (i, 0, C_SMALL // SMALL_W)),
            st,
            pl.BlockSpec((SMALL_W, hk), row),
            pl.BlockSpec((1, hk), row),
            pl.BlockSpec((1, GLA_DV), row),
        ],
        out_specs=[pl.BlockSpec((1, 1, hv), lambda i: (i, 0, 0)), st],
        out_shape=[jax.ShapeDtypeStruct((n, 1, hv), F32),
                   jax.ShapeDtypeStruct(state.shape, F32)],
        compiler_params=_cp(("arbitrary",)),
        name="gla_sample",
    )(z3, z3, z3, z3, z3, state, wa_pad, ba, gn)


PAGES_PER_ITER = 8


def _group_fold(o):
    grp = lax.broadcasted_iota(jnp.int32, (NSA_HEADS, NSA_DH), 0) // NSA_P
    out = jnp.zeros((NSA_HEADS, NSA_DH), F32)
    for g in range(NSA_G):
        out = out + jnp.where(grp == g, o[:, g * NSA_DH:(g + 1) * NSA_DH], 0.0)
    return out


def _nsa_sample_kernel(pt_ref, q_ref, slc_new_ref, win_new_ref, sm_ref, wb_ref, sl_ref,
                       phk_ref, phv_ref, pek_ref, pev_ref, kg_ref, seg_ref, m2s_ref,
                       cmp_pool, slc_pool, o_ref, wo_ref, buf, xs_scr, sh_scr, sc_scr, sem,
                       *, layer_off, n_keep):
    bi = pl.program_id(0)
    n_pages = pt_ref.shape[1]
    past = n_pages * PAGE_SIZE
    half = NSA_G * NSA_DH
    nch = past // CMP_STRIDE
    cpp = PAGE_SIZE // CMP_STRIDE

    def page_copy(pool, pg):
        return pltpu.make_async_copy(pool.at[layer_off + pt_ref[bi, pg]], buf.at[pg], sem)

    def gather_start(pool):
        def start(pg, c):
            page_copy(pool, pg).start()
            return c

        lax.fori_loop(0, n_pages, start, 0)

    def gather_wait(pool):
        def wait(pg, c):
            page_copy(pool, pg).wait()
            return c

        lax.fori_loop(0, n_pages, wait, 0)

    def transpose_pool_half(row0):
        def fill(it, c):
            for u in range(PAGES_PER_ITER):
                pg = it * PAGES_PER_ITER + u
                xt = buf[pg, row0:row0 + half, :].T
                r0 = pl.multiple_of(pg * PAGE_SIZE, PAGE_SIZE)
                xs_scr[0, pl.ds(r0, PAGE_SIZE), :] = xt[:, 0:128]
                xs_scr[1, pl.ds(r0, PAGE_SIZE), :] = xt[:, 128:256]
            return c

        lax.fori_loop(0, n_pages // PAGES_PER_ITER, fill, 0)

    def chunk_rows(s):
        rows = pl.ds(s, nch, stride=CMP_STRIDE)
        return jnp.concatenate([xs_scr[0, rows, :], xs_scr[1, rows, :]], axis=1)

    q16 = q_ref[0]
    head = lax.broadcasted_iota(jnp.int32, (NSA_HEADS, half), 0)
    lane = lax.broadcasted_iota(jnp.int32, (NSA_HEADS, half), 1)
    qblk = jnp.where(lane // NSA_DH == head // NSA_P,
                     jnp.concatenate([q16] * NSA_G, axis=1), jnp.zeros((), BF16))
    slope = sl_ref[...]
    sig = jax.nn.sigmoid(sm_ref[0])
    lane_s = lax.broadcasted_iota(jnp.int32, (NSA_HEADS, SMALL_W), 1)
    head_s = lax.broadcasted_iota(jnp.int32, (NSA_HEADS, SMALL_W), 0)

    def gate_col(c):
        return jnp.sum(jnp.where(lane_s == GATE_OFF + 3 * head_s + c, sig, 0.0),
                       axis=1, keepdims=True)

    gather_start(cmp_pool)
    gather_wait(cmp_pool)
    transpose_pool_half(0)
    kc = _compress_half(chunk_rows, nch, phk_ref, pek_ref, sh_scr)
    transpose_pool_half(half)
    gather_start(slc_pool)
    vc = _compress_half(chunk_rows, nch, phv_ref, pev_ref, sh_scr)
    kn = kc * lax.rsqrt(_seg_mean_sq(kc, seg_ref[...]) + EPS) * kg_ref[0:1, :]
    s = _nt(qblk, kn.astype(BF16))
    c_end = lax.broadcasted_iota(jnp.int32, (NSA_HEADS, nch), 1) * CMP_STRIDE + (CMP_BLOCK - 1)
    mask = c_end <= past
    sc = jnp.where(mask, s - slope * (past - c_end).astype(F32), NEG)
    m = jnp.max(sc, axis=-1, keepdims=True)
    e = jnp.where(mask, jnp.exp(sc - m), 0.0)
    p = e / jnp.maximum(jnp.sum(e, axis=-1, keepdims=True), TINY)
    o_cmp = _group_fold(_dot(p.astype(BF16), vc.astype(BF16)))
    ps_rows = []
    for g in range(NSA_G):
        r = p[g * NSA_P:g * NSA_P + 1]
        for pp in range(1, NSA_P):
            r = r + p[g * NSA_P + pp:g * NSA_P + pp + 1]
        ps_rows.append(r)
    ps = jnp.concatenate(ps_rows + [jnp.zeros((8 - NSA_G, nch), F32)], axis=0)
    hi = ps.astype(BF16)
    lo = (ps - hi.astype(F32)).astype(BF16)
    imp = _dot(hi, m2s_ref[...]) + _dot(lo, m2s_ref[...])
    nselp = imp.shape[1]
    cur = past // SEL_BLOCK
    jl = lax.broadcasted_iota(jnp.int32, (8, nselp), 1)
    forced = (jl == 0) | (jl == cur) | (jl == cur - 1)
    score = jnp.where(jl <= cur, jnp.where(forced, FORCE_SCORE, imp), NEG)
    jr = lax.broadcasted_iota(jnp.int32, (nselp, nselp), 0)
    jc = lax.broadcasted_iota(jnp.int32, (nselp, nselp), 1)
    sel_rows = []
    for g in range(NSA_G):
        row = score[g:g + 1]
        col = _row_to_col(row)
        ahead = jnp.where(jc > jr, jnp.where(col >= row, 1.0, 0.0), jnp.where(col > row, 1.0, 0.0))
        rank = jnp.sum(ahead, axis=0, keepdims=True)
        sel_g = jnp.where(rank < n_keep, 1.0, 0.0)
        sel_rows += [sel_g] * NSA_P
    sel16 = jnp.concatenate(sel_rows, axis=0).astype(BF16)

    qf = qblk.astype(F32)

    def self_score(row):
        return jnp.sum(qf * row[:, 0:half].astype(BF16).astype(F32), axis=-1, keepdims=True)

    def add_self(m, l, acc, s_new, row):
        v_new = row[:, half:2 * half].astype(BF16).astype(F32)
        m_new = jnp.maximum(m, s_new)
        alpha = jnp.exp(m - m_new)
        e_new = jnp.exp(s_new - m_new)
        l = alpha * l + e_new
        acc = alpha * acc + e_new.astype(BF16).astype(F32) * v_new
        return l, acc

    gather_wait(slc_pool)
    lane_p = lax.broadcasted_iota(jnp.int32, (NSA_HEADS, PAGE_SIZE), 1)
    j_i = lax.broadcasted_iota(jnp.int32, (nselp, PAGE_SIZE), 0)
    blk_in_page = lax.broadcasted_iota(jnp.int32, (nselp, PAGE_SIZE), 1) // SEL_BLOCK
    bpp = PAGE_SIZE // SEL_BLOCK

    def score(it, c):
        for u in range(PAGES_PER_ITER):
            pg = it * PAGES_PER_ITER + u
            s = _dot(qblk, buf[pg, 0:half, :].astype(BF16))
            in_page = (j_i == pg * bpp + blk_in_page).astype(BF16)
            picked = _dot(sel16, in_page) > 0.5
            dist = (past - (pg * PAGE_SIZE + lane_p)).astype(F32)
            sc_scr[pg] = jnp.where(picked, s - slope * dist, NEG)
        return c

    lax.fori_loop(0, n_pages // PAGES_PER_ITER, score, 0)
    s_self = self_score(slc_new_ref[0])
    sc_all = sc_scr[...]
    m = jnp.max(jnp.max(sc_all, axis=0), axis=-1, keepdims=True)
    m_all = jnp.maximum(m, s_self)
    e_all = jnp.exp(sc_all - m_all[None])
    l = jnp.sum(jnp.sum(e_all, axis=0), axis=-1, keepdims=True)
    sc_scr[...] = e_all

    def weigh(it, acc):
        parts = []
        for u in range(PAGES_PER_ITER):
            pg = it * PAGES_PER_ITER + u
            parts.append(_nt(sc_scr[pg].astype(BF16), buf[pg, half:2 * half, :].astype(BF16)))
        while len(parts) > 1:
            parts = [a + b for a, b in zip(parts[0::2], parts[1::2])]
        return acc + parts[0]

    acc = lax.fori_loop(0, n_pages // PAGES_PER_ITER, weigh, jnp.zeros((NSA_HEADS, half), F32))
    l, acc = add_self(m_all, l, acc, s_self, slc_new_ref[0])
    o_slc = _group_fold(acc / jnp.maximum(l, TINY))

    w_len = wb_ref.shape[1]
    kk = wb_ref[0, :, 0:half].astype(BF16)
    vv = wb_ref[0, :, half:2 * half].astype(BF16)
    s = _nt(qblk, kk)
    dist = w_len - lax.broadcasted_iota(jnp.int32, (NSA_HEADS, w_len), 1)
    mask = dist <= WINDOW
    sc = jnp.where(mask, s - slope * dist.astype(F32), NEG)
    m = jnp.max(sc, axis=-1, keepdims=True)
    e = jnp.where(mask, jnp.exp(sc - m), 0.0)
    l = jnp.sum(e, axis=-1, keepdims=True)
    acc = _dot(e.astype(BF16), vv)
    l, acc = add_self(m, l, acc, self_score(win_new_ref[0]), win_new_ref[0])
    o_win = _group_fold(acc / jnp.maximum(l, TINY))

    o_ref[0] = gate_col(0) * o_cmp + gate_col(1) * o_slc + gate_col(2) * o_win
    wo_ref[0, 0:w_len - 1, :] = wb_ref[0, 1:w_len, :]
    wo_ref[0, w_len - 1:w_len, :] = win_new_ref[0]


def _nsa_sample(page_table, q_bhd, slc_new, win_new, z3, win_buf, cmp_pool, slc_pool, layer_off,
                p, seg):
    n, n_pages = page_table.shape
    past = n_pages * PAGE_SIZE
    half = NSA_G * NSA_DH
    nch = past // CMP_STRIDE
    n_sel = past // SEL_BLOCK + 1
    nselp = -(-n_sel // 128) * 128
    w_len = win_buf.shape[1]
    m2s = jnp.pad(_cmp_to_sel_t(nch, nch - 1, n_sel).T, ((0, 0), (0, nselp - n_sel)))
    slope16 = _alibi_slopes().reshape(NSA_HEADS, 1)
    c2 = lambda i, pt: (0, 0)
    c3 = lambda i, pt: (0, 0, 0)
    once = pl.Buffered(1)
    grid_spec = pltpu.PrefetchScalarGridSpec(
        num_scalar_prefetch=1,
        grid=(n,),
        in_specs=[
            pl.BlockSpec((1, NSA_HEADS, NSA_DH), lambda i, pt: (i, 0, 0)),
            pl.BlockSpec((1, 1, KV_W), lambda i, pt: (i, 0, 0)),
            pl.BlockSpec((1, 1, KV_W), lambda i, pt: (i, 0, 0)),
            pl.BlockSpec((1, 1, SMALL_W), lambda i, pt: (i, 0, C_SMALL // SMALL_W)),
            pl.BlockSpec((1, w_len, KV_W), lambda i, pt: (i, 0, 0)),
            pl.BlockSpec((NSA_HEADS, 1), c2),
            pl.BlockSpec((CMP_BLOCK, half, half), c3, pipeline_mode=once),
            pl.BlockSpec((CMP_BLOCK, half, half), c3, pipeline_mode=once),
            pl.BlockSpec((CMP_BLOCK, half), c2),
            pl.BlockSpec((CMP_BLOCK, half), c2),
            pl.BlockSpec((3, half), c2),
            pl.BlockSpec((128, 128), c2),
            pl.BlockSpec((nch, nselp), c2),
            pl.BlockSpec(memory_space=pl.ANY),
            pl.BlockSpec(memory_space=pl.ANY),
        ],
        out_specs=[
            pl.BlockSpec((1, NSA_HEADS, NSA_DH), lambda i, pt: (i, 0, 0)),
            pl.BlockSpec((1, w_len, KV_W), lambda i, pt: (i, 0, 0)),
        ],
        scratch_shapes=[
            pltpu.VMEM((n_pages, KV_W, PAGE_SIZE), F32),
            pltpu.VMEM((2, past, 128), F32),
            pltpu.VMEM((nch + 8, half), F32),
            pltpu.VMEM((n_pages, NSA_HEADS, PAGE_SIZE), F32),
            pltpu.SemaphoreType.DMA(()),
        ],
    )
    return pl.pallas_call(
        functools.partial(_nsa_sample_kernel, layer_off=layer_off, n_keep=min(TOP_N, n_sel)),
        grid_spec=grid_spec,
        out_shape=[jax.ShapeDtypeStruct((n, NSA_HEADS, NSA_DH), F32),
                   jax.ShapeDtypeStruct((n, w_len, KV_W), F32)],
        compiler_params=_cp(("arbitrary",)),
        name="nsa_sample",
    )(page_table, q_bhd, slc_new, win_new, z3, win_buf, slope16, p['phk'], p['phv'], p['pek'],
      p['pev'], p['kg'], seg, m2s, cmp_pool, slc_pool)


def _sample_layer(x, p, l, cmp_pool, slc_pool, win_buf, state_gla, state_rglru, state_conv,
                  page_table, n_pool):
    n = x.shape[0]
    xs = x.reshape(n, D_MODEL)
    z2 = _inproj(xs, p['norm_mix'], p['w_in'])
    z3 = z2.reshape(n, 1, ZC)
    o_a, h_new = _rglru_sample(z2, state_conv, state_rglru, p['cw'], p['cb'], p['wax'], p['ba'],
                               p['bx'], p['lam'])
    o_b, s_new = _gla_sample(z3, state_gla, p['gla_wa'], p['gla_ba'], p['gla_gn'])
    seg = _seg_ones()
    q_hm, slc_n, _, _, win_n, _, _, _ = _nsa_prep(z2.reshape(1, n, ZC), p['qg'], p['kg'], seg)
    q_bhd = jnp.swapaxes(q_hm[0], 0, 1)
    slc_new = slc_n.reshape(n, 1, KV_W)
    win_new = win_n.reshape(n, 1, KV_W)
    w_len = win_buf.shape[1]
    o_c, win_state = _nsa_sample(page_table, q_bhd, slc_new, win_new, z3,
                                 win_buf.reshape(n, w_len, KV_W), cmp_pool, slc_pool, l * n_pool,
                                 p, seg)
    zero = jnp.zeros((n, D_MODEL), F32)
    x2 = _merge(xs, z2, o_a, o_b.reshape(n, D_MODEL), o_c.reshape(n, D_MODEL), zero, zero,
                p['w_rg_out'], p['w_gla_out'], p['w_nsa_out'], p['w_out'])
    y = _ffn(x2, p['norm_ffn'], p['wg'], p['wu'], p['wd']).reshape(n, 1, D_MODEL)
    kv_shape = (n, 1, 2, NSA_G, NSA_DH)
    cmp_kv = z2[:, C_NCMP:C_NCMP + KV_W].reshape(kv_shape)
    conv_new = jnp.concatenate([state_conv[:, 1:], z2[:, None, C_RGX:C_RGX + D_MODEL]], axis=1)
    return y, (cmp_kv, slc_n.reshape(kv_shape), win_state.reshape(n, w_len, 2, NSA_G, NSA_DH),
               s_new, h_new, conv_new)


def kernel(x_prompt, x_sample, cache_cmp_kv, cache_slc_kv, cache_win_kv, state_gla, state_rglru,
           state_conv, page_table, norm_mix_g, w_in, rg_conv_w, rg_conv_b, rg_w_a, rg_b_a, rg_w_x,
           rg_b_x, rg_lambda, gla_w_a, gla_b_a, gla_norm_g, nsa_q_norm_g, nsa_k_norm_g, nsa_phi_k,
           nsa_phi_v, nsa_cmp_pe, w_rg_out, w_gla_out, w_nsa_out, w_out, norm_ffn_g, w_ffn_gate,
           w_ffn_up, w_ffn_down):
    depth, n_pool = cache_cmp_kv.shape[:2]
    pool_shape = (depth * n_pool, KV_W, PAGE_SIZE)
    cmp_pool = jnp.transpose(cache_cmp_kv, (0, 1, 3, 4, 5, 2)).reshape(pool_shape)
    slc_pool = jnp.transpose(cache_slc_kv, (0, 1, 3, 4, 5, 2)).reshape(pool_shape)
    yp, ys = x_prompt, x_sample
    p_out = [[] for _ in range(6)]
    s_out = [[] for _ in range(6)]
    for l in range(depth):
        p = _layer_params(l, norm_mix_g, w_in, rg_conv_w, rg_conv_b, rg_w_a, rg_b_a, rg_w_x,
                          rg_b_x, rg_lambda, gla_w_a, gla_b_a, gla_norm_g, nsa_q_norm_g,
                          nsa_k_norm_g, nsa_phi_k, nsa_phi_v, nsa_cmp_pe, w_rg_out, w_gla_out,
                          w_nsa_out, w_out, norm_ffn_g, w_ffn_gate, w_ffn_up, w_ffn_down)
        yp, p_new = _prompt_layer(yp, p)
        ys, s_new = _sample_layer(ys, p, l, cmp_pool, slc_pool, cache_win_kv[l], state_gla[l],
                                  state_rglru[l], state_conv[l], page_table, n_pool)
        for i in range(6):
            p_out[i].append(p_new[i])
            s_out[i].append(s_new[i])
    return ((yp, ys) + tuple(jnp.stack(a) for a in p_out) + tuple(jnp.stack(a) for a in s_out))
```
